```python
import math
import jax, jax.numpy as jnp
from jax import lax
import numpy as np

D_MODEL = 1024
BATCH = 8
SEQ = 4096
DEPTH = 1

HEAD_DIM = 64
A_HEADS = 4
A_VDIM = 2 * HEAD_DIM
B_HEADS = 8
B_KV_HEADS = 2
B_GROUP = B_HEADS // B_KV_HEADS
A_OUT = A_HEADS * A_VDIM
B_OUT = B_HEADS * HEAD_DIM
D_MIX = A_OUT + B_OUT
A_QK = A_HEADS * 2 * HEAD_DIM
B_Q = B_HEADS * HEAD_DIM
B_KV = B_KV_HEADS * HEAD_DIM
SPLITS = (A_QK, A_QK, A_OUT, B_Q, B_KV, B_KV)
D_IN_PROJ = sum(SPLITS)
Q_BLOCK = 128
GRID_W = 64
ROPE_THETA = 10000.0
EPS = 1e-6
N_EXPERTS = 32
TOP_K = 4
D_EXPERT = 1024
SWIGLU_LIMIT = 7.0
SWIGLU_ALPHA = 1.702
MOE_BLOCK = 256

kernel_name = "hymba_diffattn_axialgqa_moe_encoder"


def lambda_init(layer):
    return 0.8 - 0.6 * math.exp(-0.3 * layer)


def rms_norm(x, g):
    xf = x.astype(jnp.float32)
    y = xf * lax.rsqrt(jnp.mean(xf * xf, axis=-1, keepdims=True) + EPS)
    return (y * g.astype(jnp.float32)).astype(x.dtype)


def rope_freqs(dim):
    return ROPE_THETA ** (-(jnp.arange(0, dim, 2, dtype=jnp.float32) / dim))


def apply_rope(x, ang):
    cos = jnp.cos(ang).astype(x.dtype)
    sin = jnp.sin(ang).astype(x.dtype)
    x1, x2 = jnp.split(x, 2, axis=-1)
    return jnp.concatenate([x1 * cos - x2 * sin, x2 * cos + x1 * sin], axis=-1)


def to_blocks(t):
    s, d = t.shape[-2], t.shape[-1]
    t = t.reshape(*t.shape[:-2], s // Q_BLOCK, Q_BLOCK, d)
    return jnp.moveaxis(t, -3, 0)


def from_blocks(o):
    o = jnp.moveaxis(o, 0, -3)
    return o.reshape(*o.shape[:-3], o.shape[-3] * o.shape[-2], o.shape[-1])


def diff_attention(q1, q2, k1, k2, v, lam):
    scale = HEAD_DIM ** -0.5

    def block(qs):
        qb1, qb2 = qs
        s1 = jnp.einsum('bhqd,bhkd->bhqk', qb1, k1).astype(jnp.float32) * scale
        s2 = jnp.einsum('bhqd,bhkd->bhqk', qb2, k2).astype(jnp.float32) * scale
        p = jax.nn.softmax(s1, axis=-1) - lam * jax.nn.softmax(s2, axis=-1)
        return jnp.einsum('bhqk,bhkd->bhqd', p.astype(v.dtype), v)

    return from_blocks(lax.map(block, (to_blocks(q1), to_blocks(q2))))


def gqa_attention(q, k, v):
    scale = HEAD_DIM ** -0.5

    def block(qb):
        s = jnp.einsum('bgrqd,bgkd->bgrqk', qb, k).astype(jnp.float32) * scale
        p = jax.nn.softmax(s, axis=-1)
        return jnp.einsum('bgrqk,bgkd->bgrqd', p.astype(v.dtype), v)

    return from_blocks(lax.map(block, to_blocks(q)))


def moe_ffn(h, router_w, router_b, w_gate, b_gate, w_up, b_up, w_down, b_down):
    b, s, d = h.shape
    n = b * s
    xf = h.reshape(n, d)
    logits = xf.astype(jnp.float32) @ router_w.astype(jnp.float32) + router_b.astype(jnp.float32)
    top_vals, top_idx = lax.top_k(logits, TOP_K)
    gates = jax.nn.softmax(top_vals, axis=-1)
    nk = n * TOP_K
    expert_flat = top_idx.reshape(nk)
    gates_flat = gates.reshape(nk)
    token_of = jnp.arange(nk, dtype=jnp.int32) // TOP_K
    order = jnp.argsort(expert_flat)
    sorted_e = expert_flat[order]
    counts = jnp.zeros((N_EXPERTS,), jnp.int32).at[expert_flat].add(1)
    offsets = jnp.cumsum(counts) - counts
    padded = ((counts + MOE_BLOCK - 1) // MOE_BLOCK) * MOE_BLOCK
    pad_end = jnp.cumsum(padded)
    pad_start = pad_end - padded
    rank = jnp.arange(nk, dtype=jnp.int32) - offsets[sorted_e]
    dest = pad_start[sorted_e] + rank
    n_slots = nk + N_EXPERTS * MOE_BLOCK
    n_blocks = n_slots // MOE_BLOCK
    slot_token = jnp.full((n_slots,), n, jnp.int32).at[dest].set(token_of[order])
    slot_gate = jnp.zeros((n_slots,), jnp.float32).at[dest].set(gates_flat[order])
    block_start = jnp.arange(n_blocks, dtype=jnp.int32) * MOE_BLOCK
    block_expert = jnp.clip(jnp.searchsorted(pad_end, block_start, side='right'), 0, N_EXPERTS - 1)
    xpad = jnp.concatenate([xf, jnp.zeros((1, d), xf.dtype)], axis=0)
    xs = xpad[slot_token].reshape(n_blocks, MOE_BLOCK, d)

    def expert_block(args):
        xb, e = args
        g = xb @ w_gate[e] + b_gate[e]
        u = xb @ w_up[e] + b_up[e]
        g = jnp.minimum(g, SWIGLU_LIMIT)
        u = jnp.clip(u, -SWIGLU_LIMIT, SWIGLU_LIMIT)
        mid = (u + 1.0) * (g * jax.nn.sigmoid(SWIGLU_ALPHA * g))
        return mid @ w_down[e] + b_down[e]

    ys = lax.map(expert_block, (xs, block_expert)).reshape(n_slots, d)
    out = jnp.zeros((n + 1, d), ys.dtype).at[slot_token].add(ys * slot_gate[:, None].astype(ys.dtype))
    return out[:n].reshape(b, s, d)


def setup_inputs(seed: int = 0) -> dict:
    key = jax.random.key(seed)
    ks = jax.random.split(key, 24)
    f = jnp.float32

    def nrm(k, shape, scale):
        return jax.random.normal(k, shape, f) * scale

    def gain(k, shape):
        return 1.0 + 0.02 * jax.random.normal(k, shape, f)

    L = DEPTH
    return {
        "x": jax.random.normal(ks[0], (BATCH, SEQ, D_MODEL), f),
        "norm1_g": gain(ks[1], (L, D_MODEL)),
        "w_in": nrm(ks[2], (L, D_MODEL, D_IN_PROJ), D_MODEL ** -0.5),
        "a_qnorm_g": gain(ks[3], (L, HEAD_DIM)),
        "a_knorm_g": gain(ks[4], (L, HEAD_DIM)),
        "a_lq1": nrm(ks[5], (L, HEAD_DIM), 0.1),
        "a_lk1": nrm(ks[6], (L, HEAD_DIM), 0.1),
        "a_lq2": nrm(ks[7], (L, HEAD_DIM), 0.1),
        "a_lk2": nrm(ks[8], (L, HEAD_DIM), 0.1),
        "a_subln_g": gain(ks[9], (L, A_VDIM)),
        "b_qnorm_g": gain(ks[10], (L, HEAD_DIM)),
        "b_knorm_g": gain(ks[11], (L, HEAD_DIM)),
        "w_out": nrm(ks[12], (L, D_MIX, D_MODEL), D_MIX ** -0.5),
        "norm2_g": gain(ks[13], (L, D_MODEL)),
        "router_w": nrm(ks[14], (L, D_MODEL, N_EXPERTS), D_MODEL ** -0.5),
        "router_b": nrm(ks[15], (L, N_EXPERTS), 0.01),
        "w_gate": nrm(ks[16], (L, N_EXPERTS, D_MODEL, D_EXPERT), D_MODEL ** -0.5),
        "b_gate": nrm(ks[17], (L, N_EXPERTS, D_EXPERT), 0.01),
        "w_up": nrm(ks[18], (L, N_EXPERTS, D_MODEL, D_EXPERT), D_MODEL ** -0.5),
        "b_up": nrm(ks[19], (L, N_EXPERTS, D_EXPERT), 0.01),
        "w_down": nrm(ks[20], (L, N_EXPERTS, D_EXPERT, D_MODEL), D_EXPERT ** -0.5),
        "b_down": nrm(ks[21], (L, N_EXPERTS, D_MODEL), 0.01),
    }


def reference(x, norm1_g, w_in, a_qnorm_g, a_knorm_g, a_lq1, a_lk1, a_lq2, a_lk2,
              a_subln_g, b_qnorm_g, b_knorm_g, w_out, norm2_g, router_w, router_b,
              w_gate, b_gate, w_up, b_up, w_down, b_down):
    b, s, _ = x.shape
    rows = s // GRID_W
    t = jnp.arange(s, dtype=jnp.float32)
    ang_1d = t[:, None] * rope_freqs(HEAD_DIM)[None, :]
    row = jnp.repeat(jnp.arange(rows, dtype=jnp.float32), GRID_W)
    col = jnp.tile(jnp.arange(GRID_W, dtype=jnp.float32), rows)
    f_ax = rope_freqs(HEAD_DIM // 2)
    ang_2d = jnp.concatenate([row[:, None] * f_ax[None, :], col[:, None] * f_ax[None, :]], axis=-1)
    split_idx = list(np.cumsum(SPLITS)[:-1])

    for layer in range(DEPTH):
        lam_init = lambda_init(layer)
        h = rms_norm(x, norm1_g[layer])
        proj = h @ w_in[layer]
        qa, ka, va, qb, kb, vb = jnp.split(proj, split_idx, axis=-1)

        qa = qa.reshape(b, s, A_HEADS, 2, HEAD_DIM).transpose(3, 0, 2, 1, 4)
        ka = ka.reshape(b, s, A_HEADS, 2, HEAD_DIM).transpose(3, 0, 2, 1, 4)
        va = va.reshape(b, s, A_HEADS, A_VDIM).transpose(0, 2, 1, 3)
        qa = apply_rope(rms_norm(qa, a_qnorm_g[layer]), ang_1d)
        ka = apply_rope(rms_norm(ka, a_knorm_g[layer]), ang_1d)
        lam = (jnp.exp(jnp.sum(a_lq1[layer].astype(jnp.float32) * a_lk1[layer].astype(jnp.float32)))
               - jnp.exp(jnp.sum(a_lq2[layer].astype(jnp.float32) * a_lk2[layer].astype(jnp.float32)))
               + lam_init)
        oa = diff_attention(qa[0], qa[1], ka[0], ka[1], va, lam)
        oa = rms_norm(oa, a_subln_g[layer]) * (1.0 - lam_init)
        oa = oa.transpose(0, 2, 1, 3).reshape(b, s, A_OUT)

        qb = qb.reshape(b, s, B_HEADS, HEAD_DIM).transpose(0, 2, 1, 3)
        kb = kb.reshape(b, s, B_KV_HEADS, HEAD_DIM).transpose(0, 2, 1, 3)
        vb = vb.reshape(b, s, B_KV_HEADS, HEAD_DIM).transpose(0, 2, 1, 3)
        qb = apply_rope(rms_norm(qb, b_qnorm_g[layer]), ang_2d)
        kb = apply_rope(rms_norm(kb, b_knorm_g[layer]), ang_2d)
        qb = qb.reshape(b, B_KV_HEADS, B_GROUP, s, HEAD_DIM)
        ob = gqa_attention(qb, kb, vb)
        ob = ob.reshape(b, B_HEADS, s, HEAD_DIM).transpose(0, 2, 1, 3).reshape(b, s, B_OUT)

        x = x + jnp.concatenate([oa, ob], axis=-1) @ w_out[layer]

        h2 = rms_norm(x, norm2_g[layer])
        x = x + moe_ffn(h2, router_w[layer], router_b[layer], w_gate[layer], b_gate[layer],
                        w_up[layer], b_up[layer], w_down[layer], b_down[layer])
    return x
```

```python
import functools
import math

import jax
import jax.numpy as jnp
from jax import lax
from jax.experimental import pallas as pl
from jax.experimental.pallas import tpu as pltpu

F32 = jnp.float32
BF16 = jnp.bfloat16

D_MODEL = 1024
HEAD_DIM = 64
A_HEADS = 4
B_HEADS = 8
B_KV_HEADS = 2
A_W = 512
B_KV_W = 128
D_IN_PROJ = 4 * A_W + 2 * B_KV_W
GRID_W = 64
ROPE_THETA = 10000.0
EPS = 1e-6
N_EXPERTS = 32
TOP_K = 4
D_EXPERT = 1024
SWIGLU_LIMIT = 7.0
SWIGLU_ALPHA = 1.702
LAM_INIT = 0.8 - 0.6 * math.exp(-0.3 * 0)

LANES = 128
VMEM_LIMIT = 56 * 1024 * 1024

TM_PROJ = 512
TQ = 256
TM_MOE = 512
T_ROWS = 256


def _cparams(sem):
    return pltpu.CompilerParams(dimension_semantics=sem, vmem_limit_bytes=VMEM_LIMIT)


def _head_norm_rope(p, gain, cos, sin, scale):
    tm, w = p.shape
    cw = min(w, 256)
    rep = cw // LANES
    r = lax.broadcasted_iota(jnp.int32, (cw, cw), 0) // HEAD_DIM
    c = lax.broadcasted_iota(jnp.int32, (cw, cw), 1) // HEAD_DIM
    ones_blk = jnp.where(r == c, 1.0, 0.0).astype(BF16)
    lane = lax.broadcasted_iota(jnp.int32, (tm, cw), 1)
    first_half = (lane % HEAD_DIM) < (HEAD_DIM // 2)
    gain_t = jnp.tile(gain, (1, rep))
    cos_t = jnp.tile(cos, (1, rep))
    sin_t = jnp.tile(sin, (1, rep))
    outs = []
    for c0 in range(0, w, cw):
        pc = p[:, c0:c0 + cw]
        ssum = jnp.dot((pc * pc).astype(BF16), ones_blk, preferred_element_type=F32)
        xn = pc * lax.rsqrt(ssum * (1.0 / HEAD_DIM) + EPS) * gain_t
        partner = jnp.where(first_half, pltpu.roll(xn, cw - HEAD_DIM // 2, 1),
                            pltpu.roll(xn, HEAD_DIM // 2, 1))
        y = xn * cos_t + partner * sin_t
        outs.append((y * scale).astype(BF16))
    return outs, cw


def _inproj_kernel(x_ref, g1_ref, w_ref, gqa_ref, gka_ref, gqb_ref, gkb_ref,
                   cosa_ref, sina_ref, cosb_ref, sinb_ref,
                   qa_ref, ka_ref, va_ref, qb_ref, kb_ref, vb_ref):
    x = x_ref[...]
    ms = jnp.mean(x * x, axis=-1, keepdims=True)
    h = (x * lax.rsqrt(ms + EPS) * g1_ref[...]).astype(BF16)
    proj = jnp.dot(h, w_ref[...], preferred_element_type=F32)
    q_scale = HEAD_DIM ** -0.5
    o = 0
    for ref, gain, cos, sin, scale, w in (
            (qa_ref, gqa_ref, cosa_ref, sina_ref, q_scale, A_W),
            (ka_ref, gka_ref, cosa_ref, sina_ref, 1.0, A_W),
            (va_ref, None, None, None, None, A_W),
            (qb_ref, gqb_ref, cosb_ref, sinb_ref, q_scale, A_W),
            (kb_ref, gkb_ref, cosb_ref, sinb_ref, 1.0, B_KV_W),
            (vb_ref, None, None, None, None, B_KV_W)):
        p = proj[:, o:o + w]
        if gain is None:
            ref[...] = p.astype(BF16)
        else:
            outs, cw = _head_norm_rope(p, gain[...], cos[...], sin[...], scale)
            for j, y in enumerate(outs):
                ref[:, j * cw:(j + 1) * cw] = y
        o += w


def _inproj(x2, g1, w_in, gqa, gka, gqb, gkb, cosa, sina, cosb, sinb, seq):
    n = x2.shape[0]
    tm = TM_PROJ
    per_seq = seq // tm
    row = lambda i: (i, 0)
    const = lambda i: (0, 0)
    tab = lambda i: (i % per_seq, 0)
    widths = (A_W, A_W, A_W, A_W, B_KV_W, B_KV_W)
    return pl.pallas_call(
        _inproj_kernel,
        grid=(n // tm,),
        in_specs=[pl.BlockSpec((tm, D_MODEL), row),
                  pl.BlockSpec((1, D_MODEL), const),
                  pl.BlockSpec((D_MODEL, D_IN_PROJ), const)]
                 + [pl.BlockSpec((1, LANES), const)] * 4
                 + [pl.BlockSpec((tm, LANES), tab)] * 4,
        out_specs=[pl.BlockSpec((tm, w), row) for w in widths],
        out_shape=[jax.ShapeDtypeStruct((n, w), BF16) for w in widths],
        compiler_params=_cparams(("arbitrary",)),
        name="inproj",
    )(x2, g1, w_in, gqa, gka, gqb, gkb, cosa, sina, cosb, sinb)


def _softmax_parts(s):
    m = jnp.max(s, axis=-1, keepdims=True)
    e = jnp.exp(s - m)
    return e, jnp.sum(e, axis=-1, keepdims=True)


def _attn_a_kernel(lq1_ref, lk1_ref, lq2_ref, lk2_ref, gsub_ref,
                   q_ref, kt_ref, v_ref, o_ref):
    q = q_ref[0]
    kt = kt_ref[0, 0]
    v = v_ref[0]
    lane = lax.broadcasted_iota(jnp.int32, q.shape, 1)
    zero = jnp.zeros_like(q)
    q1 = jnp.where(lane < HEAD_DIM, q, zero)
    q2 = jnp.where(lane >= HEAD_DIM, q, zero)
    e1, l1 = _softmax_parts(jnp.dot(q1, kt, preferred_element_type=F32))
    o1 = jnp.dot(e1.astype(BF16), v, preferred_element_type=F32)
    e2, l2 = _softmax_parts(jnp.dot(q2, kt, preferred_element_type=F32))
    o2 = jnp.dot(e2.astype(BF16), v, preferred_element_type=F32)
    lam = (jnp.exp(jnp.sum(lq1_ref[...] * lk1_ref[...], axis=-1, keepdims=True))
           - jnp.exp(jnp.sum(lq2_ref[...] * lk2_ref[...], axis=-1, keepdims=True))
           + LAM_INIT)
    o = o1 / l1 - lam * (o2 / l2)
    ms = jnp.mean(o * o, axis=-1, keepdims=True)
    o = o * lax.rsqrt(ms + EPS) * gsub_ref[...] * (1.0 - LAM_INIT)
    o_ref[0] = o.astype(BF16)


def _attn_a(lq1, lk1, lq2, lk2, gsub, qa, kat, va):
    b, s, _ = qa.shape
    vec = lambda w: pl.BlockSpec((1, w), lambda bi, h, i: (0, 0))
    return pl.pallas_call(
        _attn_a_kernel,
        grid=(b, A_HEADS, s // TQ),
        in_specs=[vec(HEAD_DIM)] * 4 + [vec(LANES)]
                 + [pl.BlockSpec((1, TQ, LANES), lambda bi, h, i: (bi, i, h)),
                    pl.BlockSpec((1, 1, LANES, s), lambda bi, h, i: (bi, h, 0, 0)),
                    pl.BlockSpec((1, s, LANES), lambda bi, h, i: (bi, 0, h))],
        out_specs=pl.BlockSpec((1, TQ, LANES), lambda bi, h, i: (bi, i, h)),
        out_shape=jax.ShapeDtypeStruct((b, s, A_W), BF16),
        compiler_params=_cparams(("arbitrary",) * 3),
        name="attn_a",
    )(lq1, lk1, lq2, lk2, gsub, qa, kat, va)


def _attn_b_kernel(q_ref, kt_ref, v_ref, o_ref):
    q = q_ref[0]
    kt = kt_ref[0, 0]
    v = v_ref[0, 0]
    lane = lax.broadcasted_iota(jnp.int32, q.shape, 1)
    zero = jnp.zeros_like(q)
    half = lax.broadcasted_iota(jnp.int32, (q.shape[0], LANES), 1) < HEAD_DIM
    group = B_HEADS // B_KV_HEADS
    for pair in range(group // 2):
        res = []
        for j in (2 * pair, 2 * pair + 1):
            qj = jnp.where(lane // HEAD_DIM == j, q, zero)
            e, l = _softmax_parts(jnp.dot(qj, kt, preferred_element_type=F32))
            res.append(jnp.dot(e.astype(BF16), v, preferred_element_type=F32) / l)
        o_ref[0, :, pair * LANES:(pair + 1) * LANES] = (
            jnp.where(half, res[0], res[1]).astype(BF16))


def _attn_b(qb, kbt4, vb2):
    b, s, _ = qb.shape
    gw = A_W // B_KV_HEADS
    return pl.pallas_call(
        _attn_b_kernel,
        grid=(b, B_KV_HEADS, s // TQ),
        in_specs=[pl.BlockSpec((1, TQ, gw), lambda bi, g, i: (bi, i, g)),
                  pl.BlockSpec((1, 1, gw, s), lambda bi, g, i: (bi, g, 0, 0)),
                  pl.BlockSpec((1, 1, s, LANES), lambda bi, g, i: (bi, g, 0, 0))],
        out_specs=pl.BlockSpec((1, TQ, gw), lambda bi, g, i: (bi, i, g)),
        out_shape=jax.ShapeDtypeStruct((b, s, A_W), BF16),
        compiler_params=_cparams(("arbitrary",) * 3),
        name="attn_b",
    )(qb, kbt4, vb2)


def _outproj_kernel(oa_ref, ob_ref, x_ref, wo_ref, g2_ref, rwh_ref, rwl_ref, rb_ref,
                    x1_ref, h2_ref, topi_ref, gate_ref, rank_ref, cnt_ref, carry_ref):
    i = pl.program_id(0)

    @pl.when(i == 0)
    def _():
        carry_ref[...] = jnp.zeros_like(carry_ref)

    tm = x_ref.shape[0]
    a = (jnp.dot(oa_ref[...], wo_ref[:A_W, :], preferred_element_type=F32)
         + jnp.dot(ob_ref[...], wo_ref[A_W:, :], preferred_element_type=F32))
    x1 = x_ref[...] + a
    x1_ref[...] = x1
    ms = jnp.mean(x1 * x1, axis=-1, keepdims=True)
    h2 = x1 * lax.rsqrt(ms + EPS) * g2_ref[...]
    h2_ref[...] = h2

    hi = h2.astype(BF16)
    lo = (h2 - hi.astype(F32)).astype(BF16)
    nt = (((1,), (1,)), ((), ()))
    rwh = rwh_ref[...]
    logits = (lax.dot_general(rwh, hi, nt, preferred_element_type=F32)
              + lax.dot_general(rwh, lo, nt, preferred_element_type=F32)
              + lax.dot_general(rwl_ref[...], hi, nt, preferred_element_type=F32)
              + rb_ref[...])

    eio = lax.broadcasted_iota(jnp.int32, logits.shape, 0)
    work = logits
    vals, idxs = [], []
    for _ in range(TOP_K):
        mx = jnp.max(work, axis=0, keepdims=True)
        idx = jnp.min(jnp.where(work == mx, eio, N_EXPERTS), axis=0, keepdims=True)
        vals.append(mx)
        idxs.append(idx)
        work = jnp.where(eio == idx, -jnp.inf, work)
    exps = [jnp.exp(v - vals[0]) for v in vals]
    tot = exps[0] + exps[1] + exps[2] + exps[3]

    member = jnp.zeros(logits.shape, F32)
    for idx in idxs:
        member = member + jnp.where(eio == idx, 1.0, 0.0)
    r = lax.broadcasted_iota(jnp.int32, (tm, tm), 0)
    c = lax.broadcasted_iota(jnp.int32, (tm, tm), 1)
    upper = jnp.where(r < c, 1.0, 0.0).astype(BF16)
    pos = jnp.dot(member.astype(BF16), upper, preferred_element_type=F32) + carry_ref[...]
    for k in range(TOP_K):
        topi_ref[k:k + 1, :] = idxs[k]
        gate_ref[k:k + 1, :] = exps[k] / tot
        rank_ref[k:k + 1, :] = jnp.sum(jnp.where(eio == idxs[k], pos, 0.0), axis=0,
                                       keepdims=True).astype(jnp.int32)
    carry = carry_ref[...] + jnp.sum(member, axis=1, keepdims=True)
    carry_ref[...] = carry
    cnt_ref[...] = carry.astype(jnp.int32)


def _outproj(oa, ob, x2, wo, g2, rwh, rwl, rb):
    n = x2.shape[0]
    tm = TM_PROJ
    row = lambda i: (i, 0)
    col = lambda i: (0, i)
    const = lambda i: (0, 0)
    return pl.pallas_call(
        _outproj_kernel,
        grid=(n // tm,),
        in_specs=[pl.BlockSpec((tm, A_W), row), pl.BlockSpec((tm, A_W), row),
                  pl.BlockSpec((tm, D_MODEL), row),
                  pl.BlockSpec((2 * A_W, D_MODEL), const),
                  pl.BlockSpec((1, D_MODEL), const),
                  pl.BlockSpec((N_EXPERTS, D_MODEL), const),
                  pl.BlockSpec((N_EXPERTS, D_MODEL), const),
                  pl.BlockSpec((N_EXPERTS, 1), const)],
        out_specs=[pl.BlockSpec((tm, D_MODEL), row), pl.BlockSpec((tm, D_MODEL), row),
                   pl.BlockSpec((TOP_K, tm), col), pl.BlockSpec((TOP_K, tm), col),
                   pl.BlockSpec((TOP_K, tm), col),
                   pl.BlockSpec((N_EXPERTS, 1), const)],
        out_shape=[jax.ShapeDtypeStruct((n, D_MODEL), F32),
                   jax.ShapeDtypeStruct((n, D_MODEL), F32),
                   jax.ShapeDtypeStruct((TOP_K, n), jnp.int32),
                   jax.ShapeDtypeStruct((TOP_K, n), F32),
                   jax.ShapeDtypeStruct((TOP_K, n), jnp.int32),
                   jax.ShapeDtypeStruct((N_EXPERTS, 1), jnp.int32)],
        scratch_shapes=[pltpu.VMEM((N_EXPERTS, 1), F32)],
        compiler_params=_cparams(("arbitrary",)),
        name="outproj_router",
    )(oa, ob, x2, wo, g2, rwh, rwl, rb)


def _row_copies_wait(like_ref, hbm_ref, sem):
    for _ in range(TOP_K):
        pltpu.make_async_copy(like_ref, hbm_ref.at[pl.ds(0, T_ROWS)], sem).wait()


def _dispatch_kernel(dest_ref, h2_ref, xs_in_ref, xs_ref, idx_ref, idx_sem, row_sem):
    del xs_in_ref
    i = pl.program_id(0)
    cp = pltpu.make_async_copy(dest_ref.at[i], idx_ref, idx_sem)
    cp.start()
    cp.wait()

    def body(r, carry):
        for k in range(TOP_K):
            d = idx_ref[k * T_ROWS + r]
            pltpu.make_async_copy(h2_ref.at[pl.ds(r, 1)], xs_ref.at[pl.ds(d, 1)],
                                  row_sem).start()
        return carry

    lax.fori_loop(0, T_ROWS, body, 0)
    _row_copies_wait(h2_ref, xs_ref, row_sem)


def _dispatch(dest_tiles, h2, xs_init):
    n = h2.shape[0]
    return pl.pallas_call(
        _dispatch_kernel,
        grid=(n // T_ROWS,),
        in_specs=[pl.BlockSpec(memory_space=pl.ANY),
                  pl.BlockSpec((T_ROWS, D_MODEL), lambda i: (i, 0)),
                  pl.BlockSpec(memory_space=pl.ANY)],
        out_specs=pl.BlockSpec(memory_space=pl.ANY),
        out_shape=jax.ShapeDtypeStruct(xs_init.shape, xs_init.dtype),
        input_output_aliases={2: 0},
        scratch_shapes=[pltpu.SMEM((TOP_K * T_ROWS,), jnp.int32),
                        pltpu.SemaphoreType.DMA, pltpu.SemaphoreType.DMA],
        compiler_params=_cparams(("arbitrary",)),
        name="dispatch",
    )(dest_tiles, h2, xs_init)


def _moe_kernel(be_ref, nused_ref, xs_ref, wg_ref, bg_ref, wu_ref, bu_ref, wd_ref, bd_ref,
                ys_ref):
    del be_ref
    i = pl.program_id(0)

    @pl.when(i < nused_ref[0])
    def _():
        x = xs_ref[...].astype(BF16)
        g = jnp.dot(x, wg_ref[0], preferred_element_type=F32) + bg_ref[0]
        u = jnp.dot(x, wu_ref[0], preferred_element_type=F32) + bu_ref[0]
        g = jnp.minimum(g, SWIGLU_LIMIT)
        u = jnp.clip(u, -SWIGLU_LIMIT, SWIGLU_LIMIT)
        mid = (u + 1.0) * (g * (1.0 / (1.0 + jnp.exp(-SWIGLU_ALPHA * g))))
        ys_ref[...] = (jnp.dot(mid.astype(BF16), wd_ref[0], preferred_element_type=F32)
                       + bd_ref[0])

    @pl.when(i >= nused_ref[0])
    def _():
        ys_ref[...] = jnp.zeros_like(ys_ref)


def _moe(block_expert, n_used, xs, wg, bg, wu, bu, wd, bd):
    n_slots = xs.shape[0]
    blk = lambda i, be, nu: (i, 0)
    wsel = lambda i, be, nu: (be[i], 0, 0)
    wspec = pl.BlockSpec((1, D_MODEL, D_EXPERT), wsel)
    bspec = pl.BlockSpec((1, 1, D_EXPERT), wsel)
    return pl.pallas_call(
        _moe_kernel,
        grid_spec=pltpu.PrefetchScalarGridSpec(
            num_scalar_prefetch=2,
            grid=(n_slots // TM_MOE,),
            in_specs=[pl.BlockSpec((TM_MOE, D_MODEL), blk),
                      wspec, bspec, wspec, bspec, wspec, bspec],
            out_specs=pl.BlockSpec((TM_MOE, D_MODEL), blk)),
        out_shape=jax.ShapeDtypeStruct((n_slots, D_MODEL), F32),
        compiler_params=_cparams(("arbitrary",)),
        name="moe_experts",
    )(block_expert, n_used, xs, wg, bg, wu, bu, wd, bd)


def _combine_kernel(dest_ref, x1_ref, gate_ref, ys_ref, out_ref, idx_ref, buf_ref,
                    idx_sem, row_sem):
    i = pl.program_id(0)
    cp = pltpu.make_async_copy(dest_ref.at[i], idx_ref, idx_sem)
    cp.start()
    cp.wait()

    def body(r, carry):
        for k in range(TOP_K):
            d = idx_ref[k * T_ROWS + r]
            pltpu.make_async_copy(ys_ref.at[pl.ds(d, 1)], buf_ref.at[k, pl.ds(r, 1)],
                                  row_sem).start()
        return carry

    lax.fori_loop(0, T_ROWS, body, 0)
    _row_copies_wait(buf_ref.at[0], ys_ref, row_sem)
    gate = gate_ref[...]
    acc = x1_ref[...]
    for k in range(TOP_K):
        acc = acc + gate[:, k:k + 1] * buf_ref[k]
    out_ref[...] = acc


def _combine(dest_tiles, x1, gates_t, ys):
    n = x1.shape[0]
    row = lambda i: (i, 0)
    return pl.pallas_call(
        _combine_kernel,
        grid=(n // T_ROWS,),
        in_specs=[pl.BlockSpec(memory_space=pl.ANY),
                  pl.BlockSpec((T_ROWS, D_MODEL), row),
                  pl.BlockSpec((T_ROWS, TOP_K), row),
                  pl.BlockSpec(memory_space=pl.ANY)],
        out_specs=pl.BlockSpec((T_ROWS, D_MODEL), row),
        out_shape=jax.ShapeDtypeStruct((n, D_MODEL), F32),
        scratch_shapes=[pltpu.SMEM((TOP_K * T_ROWS,), jnp.int32),
                        pltpu.VMEM((TOP_K, T_ROWS, D_MODEL), F32),
                        pltpu.SemaphoreType.DMA, pltpu.SemaphoreType.DMA],
        compiler_params=_cparams(("arbitrary",)),
        name="combine",
    )(dest_tiles, x1, gates_t, ys)


def _rope_tables(seq):
    def freqs(dim):
        return ROPE_THETA ** (-(jnp.arange(0, dim, 2, dtype=F32) / dim))

    def table(ang):
        cos = jnp.cos(ang)
        sin = jnp.sin(ang)
        cos64 = jnp.concatenate([cos, cos], axis=-1)
        sin64 = jnp.concatenate([-sin, sin], axis=-1)
        return jnp.tile(cos64, (1, 2)), jnp.tile(sin64, (1, 2))

    t = jnp.arange(seq, dtype=F32)
    ang_1d = t[:, None] * freqs(HEAD_DIM)[None, :]
    rows = seq // GRID_W
    rowp = jnp.repeat(jnp.arange(rows, dtype=F32), GRID_W)
    colp = jnp.tile(jnp.arange(GRID_W, dtype=F32), rows)
    f_ax = freqs(HEAD_DIM // 2)
    ang_2d = jnp.concatenate([rowp[:, None] * f_ax[None, :], colp[:, None] * f_ax[None, :]],
                             axis=-1)
    return table(ang_1d) + table(ang_2d)


def _tile2(g):
    return jnp.tile(g.astype(F32), 2)[None, :]


def kernel(x, norm1_g, w_in, a_qnorm_g, a_knorm_g, a_lq1, a_lk1, a_lq2, a_lk2, a_subln_g,
           b_qnorm_g, b_knorm_g, w_out, norm2_g, router_w, router_b,
           w_gate, b_gate, w_up, b_up, w_down, b_down):
    b, s, d = x.shape
    n = b * s
    layer = 0
    x2 = x.reshape(n, d)
    cosa, sina, cosb, sinb = _rope_tables(s)

    qa, ka, va, qb, kb, vb = _inproj(
        x2, norm1_g[layer][None, :], w_in[layer].astype(BF16),
        _tile2(a_qnorm_g[layer]), _tile2(a_knorm_g[layer]),
        _tile2(b_qnorm_g[layer]), _tile2(b_knorm_g[layer]),
        cosa, sina, cosb, sinb, s)

    kat = ka.reshape(b, s, A_HEADS, 2 * HEAD_DIM).transpose(0, 2, 3, 1)
    kbt = kb.reshape(b, s, B_KV_HEADS, HEAD_DIM).transpose(0, 2, 3, 1)
    kbt4 = jnp.tile(kbt, (1, 1, B_HEADS // B_KV_HEADS, 1))
    vbg = vb.reshape(b, s, B_KV_HEADS, HEAD_DIM).transpose(0, 2, 1, 3)
    vb2 = jnp.concatenate([vbg, vbg], axis=-1)

    oa = _attn_a(a_lq1[layer][None, :], a_lk1[layer][None, :], a_lq2[layer][None, :],
                 a_lk2[layer][None, :], a_subln_g[layer][None, :],
                 qa.reshape(b, s, A_W), kat, va.reshape(b, s, A_W))
    ob = _attn_b(qb.reshape(b, s, A_W), kbt4, vb2)

    rwt = router_w[layer].T.astype(F32)
    rwh = rwt.astype(BF16)
    rwl = (rwt - rwh.astype(F32)).astype(BF16)
    x1, h2, topi, gates, rank, counts = _outproj(
        oa.reshape(n, A_W), ob.reshape(n, A_W), x2, w_out[layer].astype(BF16),
        norm2_g[layer][None, :], rwh, rwl, router_b[layer].astype(F32)[:, None])

    counts = counts[:, 0]
    padded = ((counts + TM_MOE - 1) // TM_MOE) * TM_MOE
    pad_end = jnp.cumsum(padded)
    pad_start = pad_end - padded
    dest = pad_start[topi] + rank
    n_tiles = n // T_ROWS
    dest_tiles = dest.reshape(TOP_K, n_tiles, T_ROWS).transpose(1, 0, 2).reshape(
        n_tiles, TOP_K * T_ROWS)
    n_slots = n * TOP_K + N_EXPERTS * TM_MOE
    n_blocks = n_slots // TM_MOE
    block_start = jnp.arange(n_blocks, dtype=jnp.int32) * TM_MOE
    block_expert = jnp.minimum(
        jnp.sum((block_start[:, None] >= pad_end[None, :]).astype(jnp.int32), axis=1),
        N_EXPERTS - 1)
    n_used = (pad_end[-1:] // TM_MOE).astype(jnp.int32)

    xs = _dispatch(dest_tiles, h2, jnp.zeros((n_slots, d), F32))
    ys = _moe(block_expert, n_used, xs,
              w_gate[layer].astype(BF16), b_gate[layer][:, None, :],
              w_up[layer].astype(BF16), b_up[layer][:, None, :],
              w_down[layer].astype(BF16), b_down[layer][:, None, :])
    out = _combine(dest_tiles, x1, gates.T, ys)
    return out.reshape(b, s, d)
```

```python
import math

import jax
import jax.numpy as jnp
from jax import lax
from jax.experimental import pallas as pl
from jax.experimental.pallas import tpu as pltpu

F32 = jnp.float32
BF16 = jnp.bfloat16

D_MODEL = 1024
HEAD_DIM = 64
A_HEADS = 4
B_HEADS = 8
B_KV_HEADS = 2
B_GROUP = B_HEADS // B_KV_HEADS
A_W = 512
B_KV_W = 128
D_IN_PROJ = 4 * A_W + 2 * B_KV_W
GRID_W = 64
ROPE_THETA = 10000.0
EPS = 1e-6
N_EXPERTS = 32
TOP_K = 4
D_EXPERT = 1024
SWIGLU_LIMIT = 7.0
SWIGLU_ALPHA = 1.702
LAM_INIT = 0.8 - 0.6 * math.exp(-0.3 * 0)

LANES = 128
VMEM_LIMIT = 56 * 1024 * 1024

TM_PROJ = 512
TQ = 256
TM_MOE = 512
T_ROWS = 256
ZERO_CHUNK = 64


def _cparams(sem):
    return pltpu.CompilerParams(dimension_semantics=sem, vmem_limit_bytes=VMEM_LIMIT)


def _head_norm_rope(p, gain, cos, sin, scale):
    tm, w = p.shape
    cw = min(w, 256)
    rep = cw // LANES
    r = lax.broadcasted_iota(jnp.int32, (cw, cw), 0) // HEAD_DIM
    c = lax.broadcasted_iota(jnp.int32, (cw, cw), 1) // HEAD_DIM
    ones_blk = jnp.where(r == c, 1.0, 0.0).astype(BF16)
    lane = lax.broadcasted_iota(jnp.int32, (tm, cw), 1)
    first_half = (lane % HEAD_DIM) < (HEAD_DIM // 2)
    gain_t = jnp.tile(gain, (1, rep))
    cos_t = jnp.tile(cos, (1, rep))
    sin_t = jnp.tile(sin, (1, rep))
    outs = []
    for c0 in range(0, w, cw):
        pc = p[:, c0:c0 + cw]
        ssum = jnp.dot((pc * pc).astype(BF16), ones_blk, preferred_element_type=F32)
        xn = pc * lax.rsqrt(ssum * (1.0 / HEAD_DIM) + EPS) * gain_t
        partner = jnp.where(first_half, pltpu.roll(xn, cw - HEAD_DIM // 2, 1),
                            pltpu.roll(xn, HEAD_DIM // 2, 1))
        outs.append((xn * cos_t + partner * sin_t) * scale)
    return outs, cw


def _inproj_kernel(x_ref, g1_ref, w_ref, gqa_ref, gka_ref, gqb_ref, gkb_ref,
                   cosa_ref, sina_ref, cosb_ref, sinb_ref,
                   qa_ref, kat_ref, va_ref, qb_ref, kbt_ref, vb_ref, wbf_ref):
    @pl.when(pl.program_id(0) == 0)
    def _():
        wbf_ref[...] = w_ref[...].astype(BF16)

    x = x_ref[...]
    ms = jnp.mean(x * x, axis=-1, keepdims=True)
    h = (x * lax.rsqrt(ms + EPS) * g1_ref[...]).astype(BF16)
    proj = jnp.dot(h, wbf_ref[...], preferred_element_type=F32)
    q_scale = HEAD_DIM ** -0.5
    o = 0
    for ref, gain, cos, sin, scale, w, transposed in (
            (qa_ref, gqa_ref, cosa_ref, sina_ref, q_scale, A_W, False),
            (kat_ref, gka_ref, cosa_ref, sina_ref, 1.0, A_W, True),
            (va_ref, None, None, None, None, A_W, False),
            (qb_ref, gqb_ref, cosb_ref, sinb_ref, q_scale, A_W, False),
            (kbt_ref, gkb_ref, cosb_ref, sinb_ref, 1.0, B_KV_W, True),
            (vb_ref, None, None, None, None, B_KV_W, False)):
        p = proj[:, o:o + w]
        if gain is None:
            ref[...] = p.astype(BF16)
        else:
            outs, cw = _head_norm_rope(p, gain[...], cos[...], sin[...], scale)
            for j, y in enumerate(outs):
                if transposed:
                    ref[0, j * cw:(j + 1) * cw, :] = y.T.astype(BF16)
                else:
                    ref[:, j * cw:(j + 1) * cw] = y.astype(BF16)
        o += w


def _inproj(x2, g1, w_in, gqa, gka, gqb, gkb, cosa, sina, cosb, sinb, batch, seq):
    n = x2.shape[0]
    tm = TM_PROJ
    per_seq = seq // tm
    row = lambda i: (i, 0)
    const = lambda i: (0, 0)
    tab = lambda i: (i % per_seq, 0)
    ktr = lambda i: (i // per_seq, 0, i % per_seq)
    return pl.pallas_call(
        _inproj_kernel,
        grid=(n // tm,),
        in_specs=[pl.BlockSpec((tm, D_MODEL), row),
                  pl.BlockSpec((1, D_MODEL), const),
                  pl.BlockSpec((D_MODEL, D_IN_PROJ), const)]
                 + [pl.BlockSpec((1, LANES), const)] * 4
                 + [pl.BlockSpec((tm, LANES), tab)] * 4,
        out_specs=[pl.BlockSpec((tm, A_W), row),
                   pl.BlockSpec((1, A_W, tm), ktr),
                   pl.BlockSpec((tm, A_W), row),
                   pl.BlockSpec((tm, A_W), row),
                   pl.BlockSpec((1, B_KV_W, tm), ktr),
                   pl.BlockSpec((tm, B_KV_W), row)],
        out_shape=[jax.ShapeDtypeStruct((n, A_W), BF16),
                   jax.ShapeDtypeStruct((batch, A_W, seq), BF16),
                   jax.ShapeDtypeStruct((n, A_W), BF16),
                   jax.ShapeDtypeStruct((n, A_W), BF16),
                   jax.ShapeDtypeStruct((batch, B_KV_W, seq), BF16),
                   jax.ShapeDtypeStruct((n, B_KV_W), BF16)],
        scratch_shapes=[pltpu.VMEM((D_MODEL, D_IN_PROJ), BF16)],
        compiler_params=_cparams(("arbitrary",)),
        name="inproj",
    )(x2, g1, w_in, gqa, gka, gqb, gkb, cosa, sina, cosb, sinb)


def _softmax_parts(s):
    m = jnp.max(s, axis=-1, keepdims=True)
    e = jnp.exp(s - m)
    return e, jnp.sum(e, axis=-1, keepdims=True)


def _attn_a_kernel(lq1_ref, lk1_ref, lq2_ref, lk2_ref, gsub_ref,
                   q_ref, kt_ref, v_ref, o_ref):
    q = q_ref[0]
    v = v_ref[0]
    q1 = q[:, :HEAD_DIM]
    q2 = q[:, HEAD_DIM:]
    k1t = kt_ref[0, :HEAD_DIM, :]
    k2t = kt_ref[0, HEAD_DIM:, :]
    e1, l1 = _softmax_parts(jnp.dot(q1, k1t, preferred_element_type=F32))
    o1 = jnp.dot(e1.astype(BF16), v, preferred_element_type=F32)
    e2, l2 = _softmax_parts(jnp.dot(q2, k2t, preferred_element_type=F32))
    o2 = jnp.dot(e2.astype(BF16), v, preferred_element_type=F32)
    lam = (jnp.exp(jnp.sum(lq1_ref[...] * lk1_ref[...], axis=-1, keepdims=True))
           - jnp.exp(jnp.sum(lq2_ref[...] * lk2_ref[...], axis=-1, keepdims=True))
           + LAM_INIT)
    o = o1 / l1 - lam * (o2 / l2)
    ms = jnp.mean(o * o, axis=-1, keepdims=True)
    o = o * lax.rsqrt(ms + EPS) * gsub_ref[...] * (1.0 - LAM_INIT)
    o_ref[0] = o.astype(BF16)


def _attn_a(lq1, lk1, lq2, lk2, gsub, qa, kat, va):
    b, s, _ = qa.shape
    vec = lambda w: pl.BlockSpec((1, w), lambda bi, h, i: (0, 0))
    return pl.pallas_call(
        _attn_a_kernel,
        grid=(b, A_HEADS, s // TQ),
        in_specs=[vec(HEAD_DIM)] * 4 + [vec(LANES)]
                 + [pl.BlockSpec((1, TQ, LANES), lambda bi, h, i: (bi, i, h)),
                    pl.BlockSpec((1, LANES, s), lambda bi, h, i: (bi, h, 0)),
                    pl.BlockSpec((1, s, LANES), lambda bi, h, i: (bi, 0, h))],
        out_specs=pl.BlockSpec((1, TQ, LANES), lambda bi, h, i: (bi, i, h)),
        out_shape=jax.ShapeDtypeStruct((b, s, A_W), BF16),
        compiler_params=_cparams(("arbitrary",) * 3),
        name="attn_a",
    )(lq1, lk1, lq2, lk2, gsub, qa, kat, va)


def _attn_b_kernel(q_ref, kt_ref, v_ref, o_ref):
    g = pl.program_id(1)
    q = q_ref[0]
    kt = kt_ref[0]
    v = v_ref[0]
    low = lax.broadcasted_iota(jnp.int32, (q.shape[0], LANES), 1) < HEAD_DIM
    for pair in range(B_GROUP // 2):
        halves = []
        for side in range(2):
            j = 2 * pair + side
            qj = q[:, j * HEAD_DIM:(j + 1) * HEAD_DIM]
            e, l = _softmax_parts(jnp.dot(qj, kt, preferred_element_type=F32))
            r = jnp.dot(e.astype(BF16), v, preferred_element_type=F32) / l
            halves.append(jnp.where(g == side, r, pltpu.roll(r, HEAD_DIM, 1)))
        o_ref[0, :, pair * LANES:(pair + 1) * LANES] = (
            jnp.where(low, halves[0], halves[1]).astype(BF16))


def _attn_b(qb, kbt, vb):
    b, s, _ = qb.shape
    gw = B_GROUP * HEAD_DIM
    return pl.pallas_call(
        _attn_b_kernel,
        grid=(b, B_KV_HEADS, s // TQ),
        in_specs=[pl.BlockSpec((1, TQ, gw), lambda bi, g, i: (bi, i, g)),
                  pl.BlockSpec((1, HEAD_DIM, s), lambda bi, g, i: (bi, g, 0)),
                  pl.BlockSpec((1, s, B_KV_W), lambda bi, g, i: (bi, 0, 0))],
        out_specs=pl.BlockSpec((1, TQ, gw), lambda bi, g, i: (bi, i, g)),
        out_shape=jax.ShapeDtypeStruct((b, s, A_W), BF16),
        compiler_params=_cparams(("arbitrary",) * 3),
        name="attn_b",
    )(qb, kbt, vb)


def _outproj_kernel(oa_ref, ob_ref, x_ref, wo_ref, g2_ref, rwh_ref, rwl_ref, rb_ref,
                    x1_ref, h2_ref, topi_ref, gate_ref, rank_ref, cnt_ref,
                    carry_ref, wbf_ref):
    i = pl.program_id(0)

    @pl.when(i == 0)
    def _():
        carry_ref[...] = jnp.zeros_like(carry_ref)
        wbf_ref[...] = wo_ref[...].astype(BF16)

    tm = x_ref.shape[0]
    a = (jnp.dot(oa_ref[...], wbf_ref[:A_W, :], preferred_element_type=F32)
         + jnp.dot(ob_ref[...], wbf_ref[A_W:, :], preferred_element_type=F32))
    x1 = x_ref[...] + a
    x1_ref[...] = x1
    ms = jnp.mean(x1 * x1, axis=-1, keepdims=True)
    h2 = x1 * lax.rsqrt(ms + EPS) * g2_ref[...]
    h2_ref[...] = h2

    hi = h2.astype(BF16)
    lo = (h2 - hi.astype(F32)).astype(BF16)
    nt = (((1,), (1,)), ((), ()))
    rwh = rwh_ref[...]
    logits = (lax.dot_general(rwh, hi, nt, preferred_element_type=F32)
              + lax.dot_general(rwh, lo, nt, preferred_element_type=F32)
              + lax.dot_general(rwl_ref[...], hi, nt, preferred_element_type=F32)
              + rb_ref[...])

    eio = lax.broadcasted_iota(jnp.int32, logits.shape, 0)
    work = logits
    vals, idxs = [], []
    for _ in range(TOP_K):
        mx = jnp.max(work, axis=0, keepdims=True)
        idx = jnp.min(jnp.where(work == mx, eio, N_EXPERTS), axis=0, keepdims=True)
        vals.append(mx)
        idxs.append(idx)
        work = jnp.where(eio == idx, -jnp.inf, work)
    exps = [jnp.exp(v - vals[0]) for v in vals]
    tot = exps[0] + exps[1] + exps[2] + exps[3]

    member = jnp.zeros(logits.shape, F32)
    for idx in idxs:
        member = member + jnp.where(eio == idx, 1.0, 0.0)
    r = lax.broadcasted_iota(jnp.int32, (tm, tm), 0)
    c = lax.broadcasted_iota(jnp.int32, (tm, tm), 1)
    upper = jnp.where(r < c, 1.0, 0.0).astype(BF16)
    pos = jnp.dot(member.astype(BF16), upper, preferred_element_type=F32) + carry_ref[...]
    for k in range(TOP_K):
        topi_ref[k:k + 1, :] = idxs[k]
        gate_ref[k:k + 1, :] = exps[k] / tot
        rank_ref[k:k + 1, :] = jnp.sum(jnp.where(eio == idxs[k], pos, 0.0), axis=0,
                                       keepdims=True).astype(jnp.int32)
    carry = carry_ref[...] + jnp.sum(member, axis=1, keepdims=True)
    carry_ref[...] = carry
    cnt_ref[...] = carry.astype(jnp.int32)


def _outproj(oa, ob, x2, wo, g2, rwh, rwl, rb):
    n = x2.shape[0]
    tm = TM_PROJ
    row = lambda i: (i, 0)
    col = lambda i: (0, i)
    const = lambda i: (0, 0)
    return pl.pallas_call(
        _outproj_kernel,
        grid=(n // tm,),
        in_specs=[pl.BlockSpec((tm, A_W), row), pl.BlockSpec((tm, A_W), row),
                  pl.BlockSpec((tm, D_MODEL), row),
                  pl.BlockSpec((2 * A_W, D_MODEL), const),
                  pl.BlockSpec((1, D_MODEL), const),
                  pl.BlockSpec((N_EXPERTS, D_MODEL), const),
                  pl.BlockSpec((N_EXPERTS, D_MODEL), const),
                  pl.BlockSpec((N_EXPERTS, 1), const)],
        out_specs=[pl.BlockSpec((tm, D_MODEL), row), pl.BlockSpec((tm, D_MODEL), row),
                   pl.BlockSpec((TOP_K, tm), col), pl.BlockSpec((TOP_K, tm), col),
                   pl.BlockSpec((TOP_K, tm), col),
                   pl.BlockSpec((N_EXPERTS, 1), const)],
        out_shape=[jax.ShapeDtypeStruct((n, D_MODEL), F32),
                   jax.ShapeDtypeStruct((n, D_MODEL), F32),
                   jax.ShapeDtypeStruct((TOP_K, n), jnp.int32),
                   jax.ShapeDtypeStruct((TOP_K, n), F32),
                   jax.ShapeDtypeStruct((TOP_K, n), jnp.int32),
                   jax.ShapeDtypeStruct((N_EXPERTS, 1), jnp.int32)],
        scratch_shapes=[pltpu.VMEM((N_EXPERTS, 1), F32),
                        pltpu.VMEM((2 * A_W, D_MODEL), BF16)],
        compiler_params=_cparams(("arbitrary",)),
        name="outproj_router",
    )(oa, ob, x2, wo, g2, rwh, rwl, rb)


def _row_copies_wait(like_ref, hbm_ref, sem):
    for _ in range(TOP_K):
        pltpu.make_async_copy(like_ref, hbm_ref.at[pl.ds(0, T_ROWS)], sem).wait()


def _dispatch_kernel(zlo_ref, zhi_ref, dest_ref, h2_ref, xs_ref,
                     idx_ref, zrow_ref, idx_sem, row_sem, zero_sem):
    i = pl.program_id(0)
    cp = pltpu.make_async_copy(dest_ref.at[i], idx_ref, idx_sem)
    cp.start()

    @pl.when(i == 0)
    def _():
        zrow_ref[...] = jnp.zeros_like(zrow_ref)
        zrow = zrow_ref.at[pl.ds(0, 1)]

        def per_expert(e, carry):
            lo = zlo_ref[e]
            hi = zhi_ref[e]

            def issue(r, c):
                pltpu.make_async_copy(zrow, xs_ref.at[pl.ds(r, 1)], zero_sem).start()
                return c

            def drain(r, c):
                pltpu.make_async_copy(zrow, xs_ref.at[pl.ds(r, 1)], zero_sem).wait()
                return c

            lax.fori_loop(lo, hi, issue, 0)
            lax.fori_loop(lo, hi, drain, 0)
            return carry

        lax.fori_loop(0, N_EXPERTS, per_expert, 0)

        first = zhi_ref[N_EXPERTS - 1] // ZERO_CHUNK
        last = xs_ref.shape[0] // ZERO_CHUNK

        def chunk_copy(c):
            start = pl.multiple_of(c * ZERO_CHUNK, ZERO_CHUNK)
            return pltpu.make_async_copy(zrow_ref, xs_ref.at[pl.ds(start, ZERO_CHUNK)], zero_sem)

        def issue_chunk(c, carry):
            chunk_copy(c).start()
            return carry

        def drain_chunk(c, carry):
            chunk_copy(c).wait()
            return carry

        lax.fori_loop(first, last, issue_chunk, 0)
        lax.fori_loop(first, last, drain_chunk, 0)

    cp.wait()

    def body(r, carry):
        for k in range(TOP_K):
            d = idx_ref[k * T_ROWS + r]
            pltpu.make_async_copy(h2_ref.at[pl.ds(r, 1)], xs_ref.at[pl.ds(d, 1)],
                                  row_sem).start()
        return carry

    lax.fori_loop(0, T_ROWS, body, 0)
    _row_copies_wait(h2_ref, xs_ref, row_sem)


def _dispatch(zero_lo, zero_hi, dest_tiles, h2, n_slots):
    n = h2.shape[0]
    return pl.pallas_call(
        _dispatch_kernel,
        grid_spec=pltpu.PrefetchScalarGridSpec(
            num_scalar_prefetch=2,
            grid=(n // T_ROWS,),
            in_specs=[pl.BlockSpec(memory_space=pl.ANY),
                      pl.BlockSpec((T_ROWS, D_MODEL), lambda i, zl, zh: (i, 0))],
            out_specs=pl.BlockSpec(memory_space=pl.ANY),
            scratch_shapes=[pltpu.SMEM((TOP_K * T_ROWS,), jnp.int32),
                            pltpu.VMEM((ZERO_CHUNK, D_MODEL), F32),
                            pltpu.SemaphoreType.DMA, pltpu.SemaphoreType.DMA,
                            pltpu.SemaphoreType.DMA]),
        out_shape=jax.ShapeDtypeStruct((n_slots, D_MODEL), F32),
        compiler_params=_cparams(("arbitrary",)),
        name="dispatch",
    )(zero_lo, zero_hi, dest_tiles, h2)


def _moe_kernel(be_ref, nused_ref, xs_ref, wg_ref, bg_ref, wu_ref, bu_ref, wd_ref, bd_ref,
                ys_ref, wg_bf, wu_bf, wd_bf):
    i = pl.program_id(0)
    used = i < nused_ref[0]
    prev = be_ref[jnp.maximum(i - 1, 0)]

    @pl.when(used & ((i == 0) | (be_ref[i] != prev)))
    def _():
        wg_bf[...] = wg_ref[0].astype(BF16)
        wu_bf[...] = wu_ref[0].astype(BF16)
        wd_bf[...] = wd_ref[0].astype(BF16)

    @pl.when(jnp.logical_not(used))
    def _():
        ys_ref[...] = jnp.zeros_like(ys_ref)

    @pl.when(used)
    def _():
        x = xs_ref[...].astype(BF16)
        g = jnp.dot(x, wg_bf[...], preferred_element_type=F32) + bg_ref[0]
        u = jnp.dot(x, wu_bf[...], preferred_element_type=F32) + bu_ref[0]
        g = jnp.minimum(g, SWIGLU_LIMIT)
        u = jnp.clip(u, -SWIGLU_LIMIT, SWIGLU_LIMIT)
        mid = (u + 1.0) * (g * (1.0 / (1.0 + jnp.exp(-SWIGLU_ALPHA * g))))
        ys_ref[...] = (jnp.dot(mid.astype(BF16), wd_bf[...], preferred_element_type=F32)
                       + bd_ref[0])


def _moe(block_expert, n_used, xs, wg, bg, wu, bu, wd, bd):
    n_slots = xs.shape[0]
    blk = lambda i, be, nu: (jnp.minimum(i, nu[0] - 1), 0)
    wsel = lambda i, be, nu: (be[jnp.minimum(i, nu[0] - 1)], 0, 0)
    wspec = pl.BlockSpec((1, D_MODEL, D_EXPERT), wsel)
    bspec = pl.BlockSpec((1, 1, D_EXPERT), wsel)
    return pl.pallas_call(
        _moe_kernel,
        grid_spec=pltpu.PrefetchScalarGridSpec(
            num_scalar_prefetch=2,
            grid=(n_slots // TM_MOE,),
            in_specs=[pl.BlockSpec((TM_MOE, D_MODEL), blk),
                      wspec, bspec, wspec, bspec, wspec, bspec],
            out_specs=pl.BlockSpec((TM_MOE, D_MODEL), lambda i, be, nu: (i, 0)),
            scratch_shapes=[pltpu.VMEM((D_MODEL, D_EXPERT), BF16)] * 3),
        out_shape=jax.ShapeDtypeStruct((n_slots, D_MODEL), F32),
        compiler_params=_cparams(("arbitrary",)),
        name="moe_experts",
    )(block_expert, n_used, xs, wg, bg, wu, bu, wd, bd)


def _combine_kernel(dest_ref, x1_ref, gate_ref, ys_ref, out_ref, idx_ref, buf_ref,
                    idx_sem, row_sem):
    i = pl.program_id(0)
    cp = pltpu.make_async_copy(dest_ref.at[i], idx_ref, idx_sem)
    cp.start()
    cp.wait()

    def body(r, carry):
        for k in range(TOP_K):
            d = idx_ref[k * T_ROWS + r]
            pltpu.make_async_copy(ys_ref.at[pl.ds(d, 1)], buf_ref.at[k, pl.ds(r, 1)],
                                  row_sem).start()
        return carry

    lax.fori_loop(0, T_ROWS, body, 0)
    _row_copies_wait(buf_ref.at[0], ys_ref, row_sem)
    gate = gate_ref[...]
    acc = x1_ref[...]
    for k in range(TOP_K):
        acc = acc + gate[:, k:k + 1] * buf_ref[k]
    out_ref[...] = acc


def _combine(dest_tiles, x1, gates_t, ys):
    n = x1.shape[0]
    row = lambda i: (i, 0)
    return pl.pallas_call(
        _combine_kernel,
        grid=(n // T_ROWS,),
        in_specs=[pl.BlockSpec(memory_space=pl.ANY),
                  pl.BlockSpec((T_ROWS, D_MODEL), row),
                  pl.BlockSpec((T_ROWS, TOP_K), row),
                  pl.BlockSpec(memory_space=pl.ANY)],
        out_specs=pl.BlockSpec((T_ROWS, D_MODEL), row),
        out_shape=jax.ShapeDtypeStruct((n, D_MODEL), F32),
        scratch_shapes=[pltpu.SMEM((TOP_K * T_ROWS,), jnp.int32),
                        pltpu.VMEM((TOP_K, T_ROWS, D_MODEL), F32),
                        pltpu.SemaphoreType.DMA, pltpu.SemaphoreType.DMA],
        compiler_params=_cparams(("arbitrary",)),
        name="combine",
    )(dest_tiles, x1, gates_t, ys)


def _rope_tables(seq):
    def freqs(dim):
        return ROPE_THETA ** (-(jnp.arange(0, dim, 2, dtype=F32) / dim))

    def table(ang):
        cos = jnp.cos(ang)
        sin = jnp.sin(ang)
        cos64 = jnp.concatenate([cos, cos], axis=-1)
        sin64 = jnp.concatenate([-sin, sin], axis=-1)
        return jnp.tile(cos64, (1, 2)), jnp.tile(sin64, (1, 2))

    t = jnp.arange(seq, dtype=F32)
    ang_1d = t[:, None] * freqs(HEAD_DIM)[None, :]
    rows = seq // GRID_W
    rowp = jnp.repeat(jnp.arange(rows, dtype=F32), GRID_W)
    colp = jnp.tile(jnp.arange(GRID_W, dtype=F32), rows)
    f_ax = freqs(HEAD_DIM // 2)
    ang_2d = jnp.concatenate([rowp[:, None] * f_ax[None, :], colp[:, None] * f_ax[None, :]],
                             axis=-1)
    return table(ang_1d) + table(ang_2d)


def _tile2(g):
    return jnp.tile(g.astype(F32), 2)[None, :]


def kernel(x, norm1_g, w_in, a_qnorm_g, a_knorm_g, a_lq1, a_lk1, a_lq2, a_lk2, a_subln_g,
           b_qnorm_g, b_knorm_g, w_out, norm2_g, router_w, router_b,
           w_gate, b_gate, w_up, b_up, w_down, b_down):
    b, s, d = x.shape
    n = b * s
    layer = 0
    x2 = x.reshape(n, d)
    cosa, sina, cosb, sinb = _rope_tables(s)

    qa, kat, va, qb, kbt, vb = _inproj(
        x2, norm1_g[layer][None, :], w_in[layer],
        _tile2(a_qnorm_g[layer]), _tile2(a_knorm_g[layer]),
        _tile2(b_qnorm_g[layer]), _tile2(b_knorm_g[layer]),
        cosa, sina, cosb, sinb, b, s)

    oa = _attn_a(a_lq1[layer][None, :], a_lk1[layer][None, :], a_lq2[layer][None, :],
                 a_lk2[layer][None, :], a_subln_g[layer][None, :],
                 qa.reshape(b, s, A_W), kat, va.reshape(b, s, A_W))
    ob = _attn_b(qb.reshape(b, s, A_W), kbt, vb.reshape(b, s, B_KV_W))

    rwt = router_w[layer].T.astype(F32)
    rwh = rwt.astype(BF16)
    rwl = (rwt - rwh.astype(F32)).astype(BF16)
    x1, h2, topi, gates, rank, counts = _outproj(
        oa.reshape(n, A_W), ob.reshape(n, A_W), x2, w_out[layer],
        norm2_g[layer][None, :], rwh, rwl, router_b[layer].astype(F32)[:, None])

    counts = counts[:, 0]
    padded = ((counts + TM_MOE - 1) // TM_MOE) * TM_MOE
    pad_end = jnp.cumsum(padded)
    pad_start = pad_end - padded
    eids = jnp.arange(N_EXPERTS, dtype=jnp.int32)
    start_of = jnp.sum(jnp.where(topi[:, :, None] == eids, pad_start, 0), axis=-1)
    dest = start_of + rank
    n_tiles = n // T_ROWS
    dest_tiles = dest.reshape(TOP_K, n_tiles, T_ROWS).transpose(1, 0, 2).reshape(
        n_tiles, TOP_K * T_ROWS)
    n_slots = n * TOP_K + N_EXPERTS * TM_MOE
    n_blocks = n_slots // TM_MOE
    block_start = jnp.arange(n_blocks, dtype=jnp.int32) * TM_MOE
    block_expert = jnp.minimum(
        jnp.sum((block_start[:, None] >= pad_end[None, :]).astype(jnp.int32), axis=1),
        N_EXPERTS - 1)
    n_used = (pad_end[-1:] // TM_MOE).astype(jnp.int32)

    xs = _dispatch(pad_start + counts, pad_end, dest_tiles, h2, n_slots)
    ys = _moe(block_expert, n_used, xs,
              w_gate[layer], b_gate[layer][:, None, :],
              w_up[layer], b_up[layer][:, None, :],
              w_down[layer], b_down[layer][:, None, :])
    out = _combine(dest_tiles, x1, gates.T, ys)
    return out.reshape(b, s, d)
```

```python
import math

import jax
import jax.numpy as jnp
from jax import lax
from jax.experimental import pallas as pl
from jax.experimental.pallas import tpu as pltpu

F32 = jnp.float32
BF16 = jnp.bfloat16

D_MODEL = 1024
HEAD_DIM = 64
A_HEADS = 4
B_HEADS = 8
B_KV_HEADS = 2
B_GROUP = B_HEADS // B_KV_HEADS
A_W = 512
B_KV_W = 128
D_IN_PROJ = 4 * A_W + 2 * B_KV_W
GRID_W = 64
ROPE_THETA = 10000.0
EPS = 1e-6
N_EXPERTS = 32
TOP_K = 4
D_EXPERT = 1024
SWIGLU_LIMIT = 7.0
SWIGLU_ALPHA = 1.702
LAM_INIT = 0.8 - 0.6 * math.exp(-0.3 * 0)

LANES = 128
VMEM_LIMIT = 56 * 1024 * 1024

TM_PROJ = 512
TQ = 256
KC = 512
A_HEADS_PER_STEP = 2
TM_MOE = 512
T_ROWS = 256
ZERO_CHUNK = 64


def _cparams(sem):
    return pltpu.CompilerParams(dimension_semantics=sem, vmem_limit_bytes=VMEM_LIMIT)


def _head_norm_rope(p, gain, cos, sin, scale):
    tm, w = p.shape
    cw = min(w, 256)
    rep = cw // LANES
    r = lax.broadcasted_iota(jnp.int32, (cw, cw), 0) // HEAD_DIM
    c = lax.broadcasted_iota(jnp.int32, (cw, cw), 1) // HEAD_DIM
    ones_blk = jnp.where(r == c, 1.0, 0.0).astype(BF16)
    lane = lax.broadcasted_iota(jnp.int32, (tm, cw), 1)
    first_half = (lane % HEAD_DIM) < (HEAD_DIM // 2)
    gain_t = jnp.tile(gain, (1, rep))
    cos_t = jnp.tile(cos, (1, rep))
    sin_t = jnp.tile(sin, (1, rep))
    outs = []
    for c0 in range(0, w, cw):
        pc = p[:, c0:c0 + cw]
        ssum = jnp.dot((pc * pc).astype(BF16), ones_blk, preferred_element_type=F32)
        xn = pc * lax.rsqrt(ssum * (1.0 / HEAD_DIM) + EPS) * gain_t
        partner = jnp.where(first_half, pltpu.roll(xn, cw - HEAD_DIM // 2, 1),
                            pltpu.roll(xn, HEAD_DIM // 2, 1))
        outs.append((xn * cos_t + partner * sin_t) * scale)
    return outs, cw


def _inproj_kernel(x_ref, g1_ref, w_ref, gqa_ref, gka_ref, gqb_ref, gkb_ref,
                   cosa_ref, sina_ref, cosb_ref, sinb_ref,
                   qa_ref, kat_ref, va_ref, qb_ref, kbt_ref, vb_ref, wbf_ref):
    @pl.when(pl.program_id(0) == 0)
    def _():
        wbf_ref[...] = w_ref[...].astype(BF16)

    x = x_ref[...]
    ms = jnp.mean(x * x, axis=-1, keepdims=True)
    h = (x * lax.rsqrt(ms + EPS) * g1_ref[...]).astype(BF16)
    proj = jnp.dot(h, wbf_ref[...], preferred_element_type=F32)
    q_scale = HEAD_DIM ** -0.5 * math.log2(math.e)
    o = 0
    for ref, gain, cos, sin, scale, w, transposed in (
            (qa_ref, gqa_ref, cosa_ref, sina_ref, q_scale, A_W, False),
            (kat_ref, gka_ref, cosa_ref, sina_ref, 1.0, A_W, True),
            (va_ref, None, None, None, None, A_W, False),
            (qb_ref, gqb_ref, cosb_ref, sinb_ref, q_scale, A_W, False),
            (kbt_ref, gkb_ref, cosb_ref, sinb_ref, 1.0, B_KV_W, True),
            (vb_ref, None, None, None, None, B_KV_W, False)):
        p = proj[:, o:o + w]
        if gain is None:
            ref[...] = p.astype(BF16)
        else:
            outs, cw = _head_norm_rope(p, gain[...], cos[...], sin[...], scale)
            for j, y in enumerate(outs):
                if transposed:
                    ref[0, j * cw:(j + 1) * cw, :] = y.T.astype(BF16)
                else:
                    ref[:, j * cw:(j + 1) * cw] = y.astype(BF16)
        o += w


def _inproj(x2, g1, w_in, gqa, gka, gqb, gkb, cosa, sina, cosb, sinb, batch, seq):
    n = x2.shape[0]
    tm = TM_PROJ
    per_seq = seq // tm
    row = lambda i: (i, 0)
    const = lambda i: (0, 0)
    tab = lambda i: (i % per_seq, 0)
    ktr = lambda i: (i // per_seq, 0, i % per_seq)
    return pl.pallas_call(
        _inproj_kernel,
        grid=(n // tm,),
        in_specs=[pl.BlockSpec((tm, D_MODEL), row),
                  pl.BlockSpec((1, D_MODEL), const),
                  pl.BlockSpec((D_MODEL, D_IN_PROJ), const)]
                 + [pl.BlockSpec((1, LANES), const)] * 4
                 + [pl.BlockSpec((tm, LANES), tab)] * 4,
        out_specs=[pl.BlockSpec((tm, A_W), row),
                   pl.BlockSpec((1, A_W, tm), ktr),
                   pl.BlockSpec((tm, A_W), row),
                   pl.BlockSpec((tm, A_W), row),
                   pl.BlockSpec((1, B_KV_W, tm), ktr),
                   pl.BlockSpec((tm, B_KV_W), row)],
        out_shape=[jax.ShapeDtypeStruct((n, A_W), BF16),
                   jax.ShapeDtypeStruct((batch, A_W, seq), BF16),
                   jax.ShapeDtypeStruct((n, A_W), BF16),
                   jax.ShapeDtypeStruct((n, A_W), BF16),
                   jax.ShapeDtypeStruct((batch, B_KV_W, seq), BF16),
                   jax.ShapeDtypeStruct((n, B_KV_W), BF16)],
        scratch_shapes=[pltpu.VMEM((D_MODEL, D_IN_PROJ), BF16)],
        compiler_params=_cparams(("arbitrary",)),
        name="inproj",
    )(x2, g1, w_in, gqa, gka, gqb, gkb, cosa, sina, cosb, sinb)


def _softmax_pv_pipeline(queries, key_chunk, value_chunk, s_ref, seq):
    tq = queries[0].shape[0]
    n_chunks = seq // KC
    lanes_per_chunk = KC // LANES

    def qk_chunk(n, c, run_max):
        s = jnp.dot(queries[n], key_chunk(n, c), preferred_element_type=F32)
        s_ref[n % 2, :, c * KC:(c + 1) * KC] = s
        for k in range(lanes_per_chunk):
            run_max = jnp.maximum(run_max, s[:, k * LANES:(k + 1) * LANES])
        return run_max

    def pv_chunk(n, c, m, run_sum, acc):
        parts = []
        for k in range(lanes_per_chunk):
            lo = c * KC + k * LANES
            e = jnp.exp2(s_ref[n % 2, :, lo:lo + LANES] - m)
            run_sum = run_sum + e
            parts.append(e.astype(BF16))
        d = jnp.dot(jnp.concatenate(parts, axis=-1), value_chunk(n, c),
                    preferred_element_type=F32)
        return run_sum, d if acc is None else acc + d

    neg = jnp.full((tq, LANES), -jnp.inf, F32)
    run_max = neg
    for c in range(n_chunks):
        run_max = qk_chunk(0, c, run_max)
    results = []
    for n in range(len(queries)):
        m = jnp.broadcast_to(jnp.max(run_max, axis=-1, keepdims=True), (tq, LANES))
        run_sum = jnp.zeros((tq, LANES), F32)
        acc = None
        run_max = neg
        for c in range(n_chunks):
            if n + 1 < len(queries):
                run_max = qk_chunk(n + 1, c, run_max)
            run_sum, acc = pv_chunk(n, c, m, run_sum, acc)
        results.append((acc, jnp.sum(run_sum, axis=-1, keepdims=True)))
    return results


def _attn_a_kernel(lq1_ref, lk1_ref, lq2_ref, lk2_ref, gsub_ref,
                   q_ref, kt_ref, v_ref, o_ref, s_ref):
    seq = v_ref.shape[1]
    q = q_ref[0]
    n_mats = 2 * A_HEADS_PER_STEP
    queries = [q[:, n * HEAD_DIM:(n + 1) * HEAD_DIM] for n in range(n_mats)]
    results = _softmax_pv_pipeline(
        queries,
        lambda n, c: kt_ref[0, n * HEAD_DIM:(n + 1) * HEAD_DIM, c * KC:(c + 1) * KC],
        lambda n, c: v_ref[0, c * KC:(c + 1) * KC, (n // 2) * LANES:(n // 2 + 1) * LANES],
        s_ref, seq)
    lam = (jnp.exp(jnp.sum(lq1_ref[...] * lk1_ref[...], axis=-1, keepdims=True))
           - jnp.exp(jnp.sum(lq2_ref[...] * lk2_ref[...], axis=-1, keepdims=True))
           + LAM_INIT)
    for h in range(A_HEADS_PER_STEP):
        (o1, l1), (o2, l2) = results[2 * h], results[2 * h + 1]
        o = o1 / l1 - lam * (o2 / l2)
        ms = jnp.mean(o * o, axis=-1, keepdims=True)
        o = o * lax.rsqrt(ms + EPS) * gsub_ref[...] * (1.0 - LAM_INIT)
        o_ref[0, :, h * LANES:(h + 1) * LANES] = o.astype(BF16)


def _attn_a(lq1, lk1, lq2, lk2, gsub, qa, kat, va):
    b, s, _ = qa.shape
    w = A_HEADS_PER_STEP * 2 * HEAD_DIM
    vec = lambda width: pl.BlockSpec((1, width), lambda bi, h, i: (0, 0))
    return pl.pallas_call(
        _attn_a_kernel,
        grid=(b, A_HEADS // A_HEADS_PER_STEP, s // TQ),
        in_specs=[vec(HEAD_DIM)] * 4 + [vec(LANES)]
                 + [pl.BlockSpec((1, TQ, w), lambda bi, h, i: (bi, i, h)),
                    pl.BlockSpec((1, w, s), lambda bi, h, i: (bi, h, 0)),
                    pl.BlockSpec((1, s, w), lambda bi, h, i: (bi, 0, h))],
        out_specs=pl.BlockSpec((1, TQ, w), lambda bi, h, i: (bi, i, h)),
        out_shape=jax.ShapeDtypeStruct((b, s, A_W), BF16),
        scratch_shapes=[pltpu.VMEM((2, TQ, s), F32)],
        compiler_params=_cparams(("arbitrary",) * 3),
        name="attn_a",
    )(lq1, lk1, lq2, lk2, gsub, qa, kat, va)


def _attn_b_kernel(q_ref, kt_ref, v_ref, o_ref, s_ref):
    g = pl.program_id(1)
    seq = v_ref.shape[1]
    q = q_ref[0]
    queries = [q[:, j * HEAD_DIM:(j + 1) * HEAD_DIM] for j in range(B_GROUP)]
    results = _softmax_pv_pipeline(
        queries,
        lambda n, c: kt_ref[0, :, c * KC:(c + 1) * KC],
        lambda n, c: v_ref[0, c * KC:(c + 1) * KC, :],
        s_ref, seq)
    low = lax.broadcasted_iota(jnp.int32, (q.shape[0], LANES), 1) < HEAD_DIM
    for pair in range(B_GROUP // 2):
        halves = []
        for side in range(2):
            acc, l = results[2 * pair + side]
            r = acc / l
            halves.append(jnp.where(g == side, r, pltpu.roll(r, HEAD_DIM, 1)))
        o_ref[0, :, pair * LANES:(pair + 1) * LANES] = (
            jnp.where(low, halves[0], halves[1]).astype(BF16))


def _attn_b(qb, kbt, vb):
    b, s, _ = qb.shape
    gw = B_GROUP * HEAD_DIM
    return pl.pallas_call(
        _attn_b_kernel,
        grid=(b, B_KV_HEADS, s // TQ),
        in_specs=[pl.BlockSpec((1, TQ, gw), lambda bi, g, i: (bi, i, g)),
                  pl.BlockSpec((1, HEAD_DIM, s), lambda bi, g, i: (bi, g, 0)),
                  pl.BlockSpec((1, s, B_KV_W), lambda bi, g, i: (bi, 0, 0))],
        out_specs=pl.BlockSpec((1, TQ, gw), lambda bi, g, i: (bi, i, g)),
        out_shape=jax.ShapeDtypeStruct((b, s, A_W), BF16),
        scratch_shapes=[pltpu.VMEM((2, TQ, s), F32)],
        compiler_params=_cparams(("arbitrary",) * 3),
        name="attn_b",
    )(qb, kbt, vb)


def _outproj_kernel(oa_ref, ob_ref, x_ref, wo_ref, g2_ref, rwh_ref, rwl_ref, rb_ref,
                    x1_ref, h2_ref, topi_ref, gate_ref, rank_ref, cnt_ref,
                    carry_ref, wbf_ref):
    i = pl.program_id(0)

    @pl.when(i == 0)
    def _():
        carry_ref[...] = jnp.zeros_like(carry_ref)
        wbf_ref[...] = wo_ref[...].astype(BF16)

    tm = x_ref.shape[0]
    a = (jnp.dot(oa_ref[...], wbf_ref[:A_W, :], preferred_element_type=F32)
         + jnp.dot(ob_ref[...], wbf_ref[A_W:, :], preferred_element_type=F32))
    x1 = x_ref[...] + a
    x1_ref[...] = x1
    ms = jnp.mean(x1 * x1, axis=-1, keepdims=True)
    h2 = x1 * lax.rsqrt(ms + EPS) * g2_ref[...]
    h2_ref[...] = h2

    hi = h2.astype(BF16)
    lo = (h2 - hi.astype(F32)).astype(BF16)
    nt = (((1,), (1,)), ((), ()))
    rwh = rwh_ref[...]
    logits = (lax.dot_general(rwh, hi, nt, preferred_element_type=F32)
              + lax.dot_general(rwh, lo, nt, preferred_element_type=F32)
              + lax.dot_general(rwl_ref[...], hi, nt, preferred_element_type=F32)
              + rb_ref[...])

    eio = lax.broadcasted_iota(jnp.int32, logits.shape, 0)
    work = logits
    vals, idxs = [], []
    for _ in range(TOP_K):
        mx = jnp.max(work, axis=0, keepdims=True)
        idx = jnp.min(jnp.where(work == mx, eio, N_EXPERTS), axis=0, keepdims=True)
        vals.append(mx)
        idxs.append(idx)
        work = jnp.where(eio == idx, -jnp.inf, work)
    exps = [jnp.exp(v - vals[0]) for v in vals]
    tot = exps[0] + exps[1] + exps[2] + exps[3]

    member = jnp.zeros(logits.shape, F32)
    for idx in idxs:
        member = member + jnp.where(eio == idx, 1.0, 0.0)
    r = lax.broadcasted_iota(jnp.int32, (tm, tm), 0)
    c = lax.broadcasted_iota(jnp.int32, (tm, tm), 1)
    upper = jnp.where(r < c, 1.0, 0.0).astype(BF16)
    pos = jnp.dot(member.astype(BF16), upper, preferred_element_type=F32) + carry_ref[...]
    for k in range(TOP_K):
        topi_ref[k:k + 1, :] = idxs[k]
        gate_ref[k:k + 1, :] = exps[k] / tot
        rank_ref[k:k + 1, :] = jnp.sum(jnp.where(eio == idxs[k], pos, 0.0), axis=0,
                                       keepdims=True).astype(jnp.int32)
    carry = carry_ref[...] + jnp.sum(member, axis=1, keepdims=True)
    carry_ref[...] = carry
    cnt_ref[...] = carry.astype(jnp.int32)


def _outproj(oa, ob, x2, wo, g2, rwh, rwl, rb):
    n = x2.shape[0]
    tm = TM_PROJ
    row = lambda i: (i, 0)
    col = lambda i: (0, i)
    const = lambda i: (0, 0)
    return pl.pallas_call(
        _outproj_kernel,
        grid=(n // tm,),
        in_specs=[pl.BlockSpec((tm, A_W), row), pl.BlockSpec((tm, A_W), row),
                  pl.BlockSpec((tm, D_MODEL), row),
                  pl.BlockSpec((2 * A_W, D_MODEL), const),
                  pl.BlockSpec((1, D_MODEL), const),
                  pl.BlockSpec((N_EXPERTS, D_MODEL), const),
                  pl.BlockSpec((N_EXPERTS, D_MODEL), const),
                  pl.BlockSpec((N_EXPERTS, 1), const)],
        out_specs=[pl.BlockSpec((tm, D_MODEL), row), pl.BlockSpec((tm, D_MODEL), row),
                   pl.BlockSpec((TOP_K, tm), col), pl.BlockSpec((TOP_K, tm), col),
                   pl.BlockSpec((TOP_K, tm), col),
                   pl.BlockSpec((N_EXPERTS, 1), const)],
        out_shape=[jax.ShapeDtypeStruct((n, D_MODEL), F32),
                   jax.ShapeDtypeStruct((n, D_MODEL), F32),
                   jax.ShapeDtypeStruct((TOP_K, n), jnp.int32),
                   jax.ShapeDtypeStruct((TOP_K, n), F32),
                   jax.ShapeDtypeStruct((TOP_K, n), jnp.int32),
                   jax.ShapeDtypeStruct((N_EXPERTS, 1), jnp.int32)],
        scratch_shapes=[pltpu.VMEM((N_EXPERTS, 1), F32),
                        pltpu.VMEM((2 * A_W, D_MODEL), BF16)],
        compiler_params=_cparams(("arbitrary",)),
        name="outproj_router",
    )(oa, ob, x2, wo, g2, rwh, rwl, rb)


def _row_copies_wait(like_ref, hbm_ref, sem):
    for _ in range(TOP_K):
        pltpu.make_async_copy(like_ref, hbm_ref.at[pl.ds(0, T_ROWS)], sem).wait()


def _dispatch_kernel(zlo_ref, zhi_ref, dest_ref, h2_ref, xs_ref,
                     idx_ref, zero_ref, idx_sem, row_sem, zero_sem):
    i = pl.program_id(0)
    slot = i % 2

    def idx_copy(tile, sl):
        return pltpu.make_async_copy(dest_ref.at[tile], idx_ref.at[sl], idx_sem.at[sl])

    @pl.when(i == 0)
    def _():
        idx_copy(0, 0).start()
        zero_ref[...] = jnp.zeros_like(zero_ref)

        def chunk_copy(c):
            start = pl.multiple_of(c * ZERO_CHUNK, ZERO_CHUNK)
            return pltpu.make_async_copy(zero_ref, xs_ref.at[pl.ds(start, ZERO_CHUNK)], zero_sem)

        def issue_chunk(c, carry):
            chunk_copy(c).start()
            return carry

        def drain_chunk(c, carry):
            chunk_copy(c).wait()
            return carry

        def per_range(e, carry):
            lo = zlo_ref[e] // ZERO_CHUNK
            hi = zhi_ref[e] // ZERO_CHUNK
            lax.fori_loop(lo, hi, issue_chunk, 0)
            lax.fori_loop(lo, hi, drain_chunk, 0)
            return carry

        lax.fori_loop(0, N_EXPERTS + 1, per_range, 0)

    @pl.when(i + 1 < pl.num_programs(0))
    def _():
        idx_copy(i + 1, 1 - slot).start()

    idx_copy(i, slot).wait()

    def body(r, carry):
        for k in range(TOP_K):
            d = idx_ref[slot, k * T_ROWS + r]
            pltpu.make_async_copy(h2_ref.at[pl.ds(r, 1)], xs_ref.at[pl.ds(d, 1)],
                                  row_sem).start()
        return carry

    lax.fori_loop(0, T_ROWS, body, 0)
    _row_copies_wait(h2_ref, xs_ref, row_sem)


def _dispatch(zero_lo, zero_hi, dest_tiles, h2, n_slots):
    n = h2.shape[0]
    return pl.pallas_call(
        _dispatch_kernel,
        grid_spec=pltpu.PrefetchScalarGridSpec(
            num_scalar_prefetch=2,
            grid=(n // T_ROWS,),
            in_specs=[pl.BlockSpec(memory_space=pl.ANY),
                      pl.BlockSpec((T_ROWS, D_MODEL), lambda i, zl, zh: (i, 0))],
            out_specs=pl.BlockSpec(memory_space=pl.ANY),
            scratch_shapes=[pltpu.SMEM((2, TOP_K * T_ROWS), jnp.int32),
                            pltpu.VMEM((ZERO_CHUNK, D_MODEL), F32),
                            pltpu.SemaphoreType.DMA((2,)), pltpu.SemaphoreType.DMA,
                            pltpu.SemaphoreType.DMA]),
        out_shape=jax.ShapeDtypeStruct((n_slots, D_MODEL), F32),
        compiler_params=_cparams(("arbitrary",)),
        name="dispatch",
    )(zero_lo, zero_hi, dest_tiles, h2)


def _moe_kernel(be_ref, nused_ref, xs_ref, wg_ref, bg_ref, wu_ref, bu_ref, wd_ref, bd_ref,
                ys_ref, wg_bf, wu_bf, wd_bf):
    i = pl.program_id(0)
    used = i < nused_ref[0]
    prev = be_ref[jnp.maximum(i - 1, 0)]

    @pl.when(used & ((i == 0) | (be_ref[i] != prev)))
    def _():
        wg_bf[...] = wg_ref[0].astype(BF16)
        wu_bf[...] = wu_ref[0].astype(BF16)
        wd_bf[...] = wd_ref[0].astype(BF16)

    @pl.when(jnp.logical_not(used))
    def _():
        ys_ref[...] = jnp.zeros_like(ys_ref)

    @pl.when(used)
    def _():
        x = xs_ref[...].astype(BF16)
        g = jnp.dot(x, wg_bf[...], preferred_element_type=F32) + bg_ref[0]
        u = jnp.dot(x, wu_bf[...], preferred_element_type=F32) + bu_ref[0]
        g = jnp.minimum(g, SWIGLU_LIMIT)
        u = jnp.clip(u, -SWIGLU_LIMIT, SWIGLU_LIMIT)
        mid = (u + 1.0) * (g * (1.0 / (1.0 + jnp.exp(-SWIGLU_ALPHA * g))))
        ys_ref[...] = (jnp.dot(mid.astype(BF16), wd_bf[...], preferred_element_type=F32)
                       + bd_ref[0])


def _moe(block_expert, n_used, xs, wg, bg, wu, bu, wd, bd):
    n_slots = xs.shape[0]
    blk = lambda i, be, nu: (jnp.minimum(i, nu[0] - 1), 0)
    wsel = lambda i, be, nu: (be[jnp.minimum(i, nu[0] - 1)], 0, 0)
    wspec = pl.BlockSpec((1, D_MODEL, D_EXPERT), wsel)
    bspec = pl.BlockSpec((1, 1, D_EXPERT), wsel)
    return pl.pallas_call(
        _moe_kernel,
        grid_spec=pltpu.PrefetchScalarGridSpec(
            num_scalar_prefetch=2,
            grid=(n_slots // TM_MOE,),
            in_specs=[pl.BlockSpec((TM_MOE, D_MODEL), blk),
                      wspec, bspec, wspec, bspec, wspec, bspec],
            out_specs=pl.BlockSpec((TM_MOE, D_MODEL), lambda i, be, nu: (i, 0)),
            scratch_shapes=[pltpu.VMEM((D_MODEL, D_EXPERT), BF16)] * 3),
        out_shape=jax.ShapeDtypeStruct((n_slots, D_MODEL), F32),
        compiler_params=_cparams(("arbitrary",)),
        name="moe_experts",
    )(block_expert, n_used, xs, wg, bg, wu, bu, wd, bd)


def _combine_kernel(dest_ref, x1_ref, gate_ref, ys_ref, out_ref, idx_ref, buf_ref,
                    idx_sem, row_sem):
    i = pl.program_id(0)
    n_tiles = pl.num_programs(0)
    slot = i % 2

    def idx_copy(tile, sl):
        return pltpu.make_async_copy(dest_ref.at[tile], idx_ref.at[sl], idx_sem.at[sl])

    def gather_rows(sl):
        def body(r, carry):
            for k in range(TOP_K):
                d = idx_ref[sl, k * T_ROWS + r]
                pltpu.make_async_copy(ys_ref.at[pl.ds(d, 1)], buf_ref.at[sl, k, pl.ds(r, 1)],
                                      row_sem.at[sl]).start()
            return carry

        lax.fori_loop(0, T_ROWS, body, 0)

    @pl.when(i == 0)
    def _():
        idx_copy(0, 0).start()
        idx_copy(0, 0).wait()
        gather_rows(0)

        @pl.when(n_tiles > 1)
        def _():
            idx_copy(1, 1).start()

    @pl.when(i + 1 < n_tiles)
    def _():
        idx_copy(i + 1, 1 - slot).wait()
        gather_rows(1 - slot)

    @pl.when(i + 2 < n_tiles)
    def _():
        idx_copy(i + 2, slot).start()

    _row_copies_wait(buf_ref.at[slot, 0], ys_ref, row_sem.at[slot])
    gate = gate_ref[...]
    acc = x1_ref[...]
    for k in range(TOP_K):
        acc = acc + gate[:, k:k + 1] * buf_ref[slot, k]
    out_ref[...] = acc


def _combine(dest_tiles, x1, gates_t, ys):
    n = x1.shape[0]
    row = lambda i: (i, 0)
    return pl.pallas_call(
        _combine_kernel,
        grid=(n // T_ROWS,),
        in_specs=[pl.BlockSpec(memory_space=pl.ANY),
                  pl.BlockSpec((T_ROWS, D_MODEL), row),
                  pl.BlockSpec((T_ROWS, TOP_K), row),
                  pl.BlockSpec(memory_space=pl.ANY)],
        out_specs=pl.BlockSpec((T_ROWS, D_MODEL), row),
        out_shape=jax.ShapeDtypeStruct((n, D_MODEL), F32),
        scratch_shapes=[pltpu.SMEM((2, TOP_K * T_ROWS), jnp.int32),
                        pltpu.VMEM((2, TOP_K, T_ROWS, D_MODEL), F32),
                        pltpu.SemaphoreType.DMA((2,)), pltpu.SemaphoreType.DMA((2,))],
        compiler_params=_cparams(("arbitrary",)),
        name="combine",
    )(dest_tiles, x1, gates_t, ys)


def _rope_tables(seq):
    def freqs(dim):
        return ROPE_THETA ** (-(jnp.arange(0, dim, 2, dtype=F32) / dim))

    def table(ang):
        cos = jnp.cos(ang)
        sin = jnp.sin(ang)
        cos64 = jnp.concatenate([cos, cos], axis=-1)
        sin64 = jnp.concatenate([-sin, sin], axis=-1)
        return jnp.tile(cos64, (1, 2)), jnp.tile(sin64, (1, 2))

    t = jnp.arange(seq, dtype=F32)
    ang_1d = t[:, None] * freqs(HEAD_DIM)[None, :]
    rows = seq // GRID_W
    rowp = jnp.repeat(jnp.arange(rows, dtype=F32), GRID_W)
    colp = jnp.tile(jnp.arange(GRID_W, dtype=F32), rows)
    f_ax = freqs(HEAD_DIM // 2)
    ang_2d = jnp.concatenate([rowp[:, None] * f_ax[None, :], colp[:, None] * f_ax[None, :]],
                             axis=-1)
    return table(ang_1d) + table(ang_2d)


def _tile2(g):
    return jnp.tile(g.astype(F32), 2)[None, :]


def kernel(x, norm1_g, w_in, a_qnorm_g, a_knorm_g, a_lq1, a_lk1, a_lq2, a_lk2, a_subln_g,
           b_qnorm_g, b_knorm_g, w_out, norm2_g, router_w, router_b,
           w_gate, b_gate, w_up, b_up, w_down, b_down):
    b, s, d = x.shape
    n = b * s
    layer = 0
    x2 = x.reshape(n, d)
    cosa, sina, cosb, sinb = _rope_tables(s)

    qa, kat, va, qb, kbt, vb = _inproj(
        x2, norm1_g[layer][None, :], w_in[layer],
        _tile2(a_qnorm_g[layer]), _tile2(a_knorm_g[layer]),
        _tile2(b_qnorm_g[layer]), _tile2(b_knorm_g[layer]),
        cosa, sina, cosb, sinb, b, s)

    oa = _attn_a(a_lq1[layer][None, :], a_lk1[layer][None, :], a_lq2[layer][None, :],
                 a_lk2[layer][None, :], a_subln_g[layer][None, :],
                 qa.reshape(b, s, A_W), kat, va.reshape(b, s, A_W))
    ob = _attn_b(qb.reshape(b, s, A_W), kbt, vb.reshape(b, s, B_KV_W))

    rwt = router_w[layer].T.astype(F32)
    rwh = rwt.astype(BF16)
    rwl = (rwt - rwh.astype(F32)).astype(BF16)
    x1, h2, topi, gates, rank, counts = _outproj(
        oa.reshape(n, A_W), ob.reshape(n, A_W), x2, w_out[layer],
        norm2_g[layer][None, :], rwh, rwl, router_b[layer].astype(F32)[:, None])

    counts = counts[:, 0]
    padded = ((counts + TM_MOE - 1) // TM_MOE) * TM_MOE
    pad_end = jnp.cumsum(padded)
    pad_start = pad_end - padded
    eids = jnp.arange(N_EXPERTS, dtype=jnp.int32)
    start_of = jnp.sum(jnp.where(topi[:, :, None] == eids, pad_start, 0), axis=-1)
    dest = start_of + rank
    n_tiles = n // T_ROWS
    dest_tiles = dest.reshape(TOP_K, n_tiles, T_ROWS).transpose(1, 0, 2).reshape(
        n_tiles, TOP_K * T_ROWS)
    n_slots = n * TOP_K + N_EXPERTS * TM_MOE
    n_blocks = n_slots // TM_MOE
    block_start = jnp.arange(n_blocks, dtype=jnp.int32) * TM_MOE
    block_expert = jnp.minimum(
        jnp.sum((block_start[:, None] >= pad_end[None, :]).astype(jnp.int32), axis=1),
        N_EXPERTS - 1)
    n_used = (pad_end[-1:] // TM_MOE).astype(jnp.int32)

    zero_lo = jnp.concatenate([jnp.maximum(pad_end - TM_MOE, pad_start), pad_end[-1:]])
    zero_hi = jnp.concatenate([pad_end, jnp.full((1,), n_slots, jnp.int32)])
    xs = _dispatch(zero_lo.astype(jnp.int32), zero_hi.astype(jnp.int32), dest_tiles, h2,
                   n_slots)
    ys = _moe(block_expert, n_used, xs,
              w_gate[layer], b_gate[layer][:, None, :],
              w_up[layer], b_up[layer][:, None, :],
              w_down[layer], b_down[layer][:, None, :])
    out = _combine(dest_tiles, x1, gates.T, ys)
    return out.reshape(b, s, d)
```

```python
import math

import jax
import jax.numpy as jnp
from jax import lax
from jax.experimental import pallas as pl
from jax.experimental.pallas import tpu as pltpu

F32 = jnp.float32
BF16 = jnp.bfloat16

D_MODEL = 1024
HEAD_DIM = 64
A_HEADS = 4
B_HEADS = 8
B_KV_HEADS = 2
B_GROUP = B_HEADS // B_KV_HEADS
A_W = 512
B_KV_W = 128
D_IN_PROJ = 4 * A_W + 2 * B_KV_W
GRID_W = 64
ROPE_THETA = 10000.0
EPS = 1e-6
N_EXPERTS = 32
TOP_K = 4
D_EXPERT = 1024
SWIGLU_LIMIT = 7.0
SWIGLU_ALPHA = 1.702
LAM_INIT = 0.8 - 0.6 * math.exp(-0.3 * 0)

LANES = 128
ROW_TILE = D_MODEL // LANES
VMEM_LIMIT = 56 * 1024 * 1024

TM_PROJ = 512
TQ = 256
KC = 512
A_HEADS_PER_STEP = 2
TM_MOE = 512
T_ROWS = 256
ZERO_CHUNK = 64

def _cparams(sem):
    return pltpu.CompilerParams(dimension_semantics=sem, vmem_limit_bytes=VMEM_LIMIT)


def _head_norm_rope(p, gain, cos, sin, scale):
    tm, w = p.shape
    cw = min(w, 256)
    rep = cw // LANES
    r = lax.broadcasted_iota(jnp.int32, (cw, cw), 0) // HEAD_DIM
    c = lax.broadcasted_iota(jnp.int32, (cw, cw), 1) // HEAD_DIM
    ones_blk = jnp.where(r == c, 1.0, 0.0).astype(BF16)
    lane = lax.broadcasted_iota(jnp.int32, (tm, cw), 1)
    first_half = (lane % HEAD_DIM) < (HEAD_DIM // 2)
    gain_t = jnp.tile(gain, (1, rep))
    cos_t = jnp.tile(cos, (1, rep))
    sin_t = jnp.tile(sin, (1, rep))
    outs = []
    for c0 in range(0, w, cw):
        pc = p[:, c0:c0 + cw]
        ssum = jnp.dot((pc * pc).astype(BF16), ones_blk, preferred_element_type=F32)
        xn = pc * lax.rsqrt(ssum * (1.0 / HEAD_DIM) + EPS) * gain_t
        partner = jnp.where(first_half, pltpu.roll(xn, cw - HEAD_DIM // 2, 1),
                            pltpu.roll(xn, HEAD_DIM // 2, 1))
        outs.append((xn * cos_t + partner * sin_t) * scale)
    return outs, cw


def _inproj_kernel(x_ref, g1_ref, w_ref, gqa_ref, gka_ref, gqb_ref, gkb_ref,
                   cosa_ref, sina_ref, cosb_ref, sinb_ref,
                   qa_ref, kat_ref, va_ref, qb_ref, kbt_ref, vb_ref, wbf_ref):
    @pl.when(pl.program_id(0) == 0)
    def _():
        wbf_ref[...] = w_ref[...].astype(BF16)

    x = x_ref[...]
    ms = jnp.mean(x * x, axis=-1, keepdims=True)
    h = (x * lax.rsqrt(ms + EPS) * g1_ref[...]).astype(BF16)
    proj = jnp.dot(h, wbf_ref[...], preferred_element_type=F32)
    q_scale = HEAD_DIM ** -0.5 * math.log2(math.e)
    o = 0
    for ref, gain, cos, sin, scale, w, transposed in (
            (qa_ref, gqa_ref, cosa_ref, sina_ref, q_scale, A_W, False),
            (kat_ref, gka_ref, cosa_ref, sina_ref, 1.0, A_W, True),
            (va_ref, None, None, None, None, A_W, False),
            (qb_ref, gqb_ref, cosb_ref, sinb_ref, q_scale, A_W, False),
            (kbt_ref, gkb_ref, cosb_ref, sinb_ref, 1.0, B_KV_W, True),
            (vb_ref, None, None, None, None, B_KV_W, False)):
        p = proj[:, o:o + w]
        if gain is None:
            ref[...] = p.astype(BF16)
        else:
            outs, cw = _head_norm_rope(p, gain[...], cos[...], sin[...], scale)
            for j, y in enumerate(outs):
                if transposed:
                    ref[0, j * cw:(j + 1) * cw, :] = y.T.astype(BF16)
                else:
                    ref[:, j * cw:(j + 1) * cw] = y.astype(BF16)
        o += w


def _inproj(x2, g1, w_in, gqa, gka, gqb, gkb, cosa, sina, cosb, sinb, batch, seq):
    n = x2.shape[0]
    tm = TM_PROJ
    per_seq = seq // tm
    row = lambda i: (i, 0)
    const = lambda i: (0, 0)
    tab = lambda i: (i % per_seq, 0)
    ktr = lambda i: (i // per_seq, 0, i % per_seq)
    return pl.pallas_call(
        _inproj_kernel,
        grid=(n // tm,),
        in_specs=[pl.BlockSpec((tm, D_MODEL), row),
                  pl.BlockSpec((1, D_MODEL), const),
                  pl.BlockSpec((D_MODEL, D_IN_PROJ), const)]
                 + [pl.BlockSpec((1, LANES), const)] * 4
                 + [pl.BlockSpec((tm, LANES), tab)] * 4,
        out_specs=[pl.BlockSpec((tm, A_W), row),
                   pl.BlockSpec((1, A_W, tm), ktr),
                   pl.BlockSpec((tm, A_W), row),
                   pl.BlockSpec((tm, A_W), row),
                   pl.BlockSpec((1, B_KV_W, tm), ktr),
                   pl.BlockSpec((tm, B_KV_W), row)],
        out_shape=[jax.ShapeDtypeStruct((n, A_W), BF16),
                   jax.ShapeDtypeStruct((batch, A_W, seq), BF16),
                   jax.ShapeDtypeStruct((n, A_W), BF16),
                   jax.ShapeDtypeStruct((n, A_W), BF16),
                   jax.ShapeDtypeStruct((batch, B_KV_W, seq), BF16),
                   jax.ShapeDtypeStruct((n, B_KV_W), BF16)],
        scratch_shapes=[pltpu.VMEM((D_MODEL, D_IN_PROJ), BF16)],
        compiler_params=_cparams(("arbitrary",)),
        name="inproj",
    )(x2, g1, w_in, gqa, gka, gqb, gkb, cosa, sina, cosb, sinb)


def _softmax_pv_pipeline(queries, key_chunk, value_chunk, s_ref, seq):
    tq = queries[0].shape[0]
    n_chunks = seq // KC
    lanes_per_chunk = KC // LANES

    def qk_chunk(n, c, run_max):
        s = jnp.dot(queries[n], key_chunk(n, c), preferred_element_type=F32)
        s_ref[n % 2, :, c * KC:(c + 1) * KC] = s
        for k in range(lanes_per_chunk):
            run_max = jnp.maximum(run_max, s[:, k * LANES:(k + 1) * LANES])
        return run_max

    def pv_chunk(n, c, m, run_sum, acc):
        parts = []
        for k in range(lanes_per_chunk):
            lo = c * KC + k * LANES
            e = jnp.exp2(s_ref[n % 2, :, lo:lo + LANES] - m)
            run_sum = run_sum + e
            parts.append(e.astype(BF16))
        d = jnp.dot(jnp.concatenate(parts, axis=-1), value_chunk(n, c),
                    preferred_element_type=F32)
        return run_sum, d if acc is None else acc + d

    neg = jnp.full((tq, LANES), -jnp.inf, F32)
    run_max = neg
    for c in range(n_chunks):
        run_max = qk_chunk(0, c, run_max)
    results = []
    for n in range(len(queries)):
        m = jnp.broadcast_to(jnp.max(run_max, axis=-1, keepdims=True), (tq, LANES))
        run_sum = jnp.zeros((tq, LANES), F32)
        acc = None
        run_max = neg
        for c in range(n_chunks):
            if n + 1 < len(queries):
                run_max = qk_chunk(n + 1, c, run_max)
            run_sum, acc = pv_chunk(n, c, m, run_sum, acc)
        results.append((acc, jnp.sum(run_sum, axis=-1, keepdims=True)))
    return results


def _attn_a_kernel(lq1_ref, lk1_ref, lq2_ref, lk2_ref, gsub_ref,
                   q_ref, kt_ref, v_ref, o_ref, s_ref):
    seq = v_ref.shape[1]
    q = q_ref[0]
    n_mats = 2 * A_HEADS_PER_STEP
    queries = [q[:, n * HEAD_DIM:(n + 1) * HEAD_DIM] for n in range(n_mats)]
    results = _softmax_pv_pipeline(
        queries,
        lambda n, c: kt_ref[0, n * HEAD_DIM:(n + 1) * HEAD_DIM, c * KC:(c + 1) * KC],
        lambda n, c: v_ref[0, c * KC:(c + 1) * KC, (n // 2) * LANES:(n // 2 + 1) * LANES],
        s_ref, seq)
    lam = (jnp.exp(jnp.sum(lq1_ref[...] * lk1_ref[...], axis=-1, keepdims=True))
           - jnp.exp(jnp.sum(lq2_ref[...] * lk2_ref[...], axis=-1, keepdims=True))
           + LAM_INIT)
    for h in range(A_HEADS_PER_STEP):
        (o1, l1), (o2, l2) = results[2 * h], results[2 * h + 1]
        o = o1 / l1 - lam * (o2 / l2)
        ms = jnp.mean(o * o, axis=-1, keepdims=True)
        o = o * lax.rsqrt(ms + EPS) * gsub_ref[...] * (1.0 - LAM_INIT)
        o_ref[0, :, h * LANES:(h + 1) * LANES] = o.astype(BF16)


def _attn_a(lq1, lk1, lq2, lk2, gsub, qa, kat, va):
    b, s, _ = qa.shape
    w = A_HEADS_PER_STEP * 2 * HEAD_DIM
    vec = lambda width: pl.BlockSpec((1, width), lambda bi, h, i: (0, 0))
    return pl.pallas_call(
        _attn_a_kernel,
        grid=(b, A_HEADS // A_HEADS_PER_STEP, s // TQ),
        in_specs=[vec(HEAD_DIM)] * 4 + [vec(LANES)]
                 + [pl.BlockSpec((1, TQ, w), lambda bi, h, i: (bi, i, h)),
                    pl.BlockSpec((1, w, s), lambda bi, h, i: (bi, h, 0)),
                    pl.BlockSpec((1, s, w), lambda bi, h, i: (bi, 0, h))],
        out_specs=pl.BlockSpec((1, TQ, w), lambda bi, h, i: (bi, i, h)),
        out_shape=jax.ShapeDtypeStruct((b, s, A_W), BF16),
        scratch_shapes=[pltpu.VMEM((2, TQ, s), F32)],
        compiler_params=_cparams(("arbitrary",) * 3),
        name="attn_a",
    )(lq1, lk1, lq2, lk2, gsub, qa, kat, va)


def _attn_b_kernel(q_ref, kt_ref, v_ref, o_ref, s_ref):
    g = pl.program_id(1)
    seq = v_ref.shape[1]
    q = q_ref[0]
    queries = [q[:, j * HEAD_DIM:(j + 1) * HEAD_DIM] for j in range(B_GROUP)]
    results = _softmax_pv_pipeline(
        queries,
        lambda n, c: kt_ref[0, :, c * KC:(c + 1) * KC],
        lambda n, c: v_ref[0, c * KC:(c + 1) * KC, :],
        s_ref, seq)
    low = lax.broadcasted_iota(jnp.int32, (q.shape[0], LANES), 1) < HEAD_DIM
    for pair in range(B_GROUP // 2):
        halves = []
        for side in range(2):
            acc, l = results[2 * pair + side]
            r = acc / l
            halves.append(jnp.where(g == side, r, pltpu.roll(r, HEAD_DIM, 1)))
        o_ref[0, :, pair * LANES:(pair + 1) * LANES] = (
            jnp.where(low, halves[0], halves[1]).astype(BF16))


def _attn_b(qb, kbt, vb):
    b, s, _ = qb.shape
    gw = B_GROUP * HEAD_DIM
    return pl.pallas_call(
        _attn_b_kernel,
        grid=(b, B_KV_HEADS, s // TQ),
        in_specs=[pl.BlockSpec((1, TQ, gw), lambda bi, g, i: (bi, i, g)),
                  pl.BlockSpec((1, HEAD_DIM, s), lambda bi, g, i: (bi, g, 0)),
                  pl.BlockSpec((1, s, B_KV_W), lambda bi, g, i: (bi, 0, 0))],
        out_specs=pl.BlockSpec((1, TQ, gw), lambda bi, g, i: (bi, i, g)),
        out_shape=jax.ShapeDtypeStruct((b, s, A_W), BF16),
        scratch_shapes=[pltpu.VMEM((2, TQ, s), F32)],
        compiler_params=_cparams(("arbitrary",) * 3),
        name="attn_b",
    )(qb, kbt, vb)


def _store_row_tiled(ref, val):
    t = val.shape[0]
    for s in range(ROW_TILE):
        ref[pl.ds(s, t, stride=ROW_TILE), :] = val[:, s * LANES:(s + 1) * LANES]


def _load_row_tiled(ref):
    t = ref.shape[0] // ROW_TILE
    return jnp.concatenate([ref[pl.ds(s, t, stride=ROW_TILE), :] for s in range(ROW_TILE)],
                           axis=1)


def _tile_rows(row, count):
    return pl.ds(pl.multiple_of(row * ROW_TILE, ROW_TILE), count * ROW_TILE)


def _outproj_kernel(oa_ref, ob_ref, x_ref, wo_ref, g2_ref, rwh_ref, rwl_ref, rb_ref,
                    x1_ref, h2_ref, topi_ref, gate_ref, rank_ref, cnt_ref,
                    carry_ref, wbf_ref):
    i = pl.program_id(0)

    @pl.when(i == 0)
    def _():
        carry_ref[...] = jnp.zeros_like(carry_ref)
        wbf_ref[...] = wo_ref[...].astype(BF16)

    tm = x_ref.shape[0]
    a = (jnp.dot(oa_ref[...], wbf_ref[:A_W, :], preferred_element_type=F32)
         + jnp.dot(ob_ref[...], wbf_ref[A_W:, :], preferred_element_type=F32))
    x1 = x_ref[...] + a
    x1_ref[...] = x1
    ms = jnp.mean(x1 * x1, axis=-1, keepdims=True)
    h2 = x1 * lax.rsqrt(ms + EPS) * g2_ref[...]
    _store_row_tiled(h2_ref, h2)

    hi = h2.astype(BF16)
    lo = (h2 - hi.astype(F32)).astype(BF16)
    nt = (((1,), (1,)), ((), ()))
    rwh = rwh_ref[...]
    logits = (lax.dot_general(rwh, hi, nt, preferred_element_type=F32)
              + lax.dot_general(rwh, lo, nt, preferred_element_type=F32)
              + lax.dot_general(rwl_ref[...], hi, nt, preferred_element_type=F32)
              + rb_ref[...])

    eio = lax.broadcasted_iota(jnp.int32, logits.shape, 0)
    work = logits
    vals, idxs = [], []
    for _ in range(TOP_K):
        mx = jnp.max(work, axis=0, keepdims=True)
        idx = jnp.min(jnp.where(work == mx, eio, N_EXPERTS), axis=0, keepdims=True)
        vals.append(mx)
        idxs.append(idx)
        work = jnp.where(eio == idx, -jnp.inf, work)
    exps = [jnp.exp(v - vals[0]) for v in vals]
    tot = exps[0] + exps[1] + exps[2] + exps[3]

    member = jnp.zeros(logits.shape, F32)
    for idx in idxs:
        member = member + jnp.where(eio == idx, 1.0, 0.0)
    r = lax.broadcasted_iota(jnp.int32, (tm, tm), 0)
    c = lax.broadcasted_iota(jnp.int32, (tm, tm), 1)
    upper = jnp.where(r < c, 1.0, 0.0).astype(BF16)
    pos = jnp.dot(member.astype(BF16), upper, preferred_element_type=F32) + carry_ref[...]
    for k in range(TOP_K):
        topi_ref[k:k + 1, :] = idxs[k]
        gate_ref[k:k + 1, :] = exps[k] / tot
        rank_ref[k:k + 1, :] = jnp.sum(jnp.where(eio == idxs[k], pos, 0.0), axis=0,
                                       keepdims=True).astype(jnp.int32)
    carry = carry_ref[...] + jnp.sum(member, axis=1, keepdims=True)
    carry_ref[...] = carry
    cnt_ref[...] = carry.astype(jnp.int32)


def _outproj(oa, ob, x2, wo, g2, rwh, rwl, rb):
    n = x2.shape[0]
    tm = TM_PROJ
    row = lambda i: (i, 0)
    col = lambda i: (0, i)
    const = lambda i: (0, 0)
    return pl.pallas_call(
        _outproj_kernel,
        grid=(n // tm,),
        in_specs=[pl.BlockSpec((tm, A_W), row), pl.BlockSpec((tm, A_W), row),
                  pl.BlockSpec((tm, D_MODEL), row),
                  pl.BlockSpec((2 * A_W, D_MODEL), const),
                  pl.BlockSpec((1, D_MODEL), const),
                  pl.BlockSpec((N_EXPERTS, D_MODEL), const),
                  pl.BlockSpec((N_EXPERTS, D_MODEL), const),
                  pl.BlockSpec((N_EXPERTS, 1), const)],
        out_specs=[pl.BlockSpec((tm, D_MODEL), row), pl.BlockSpec((tm * ROW_TILE, LANES), row),
                   pl.BlockSpec((TOP_K, tm), col), pl.BlockSpec((TOP_K, tm), col),
                   pl.BlockSpec((TOP_K, tm), col),
                   pl.BlockSpec((N_EXPERTS, 1), const)],
        out_shape=[jax.ShapeDtypeStruct((n, D_MODEL), F32),
                   jax.ShapeDtypeStruct((n * ROW_TILE, LANES), F32),
                   jax.ShapeDtypeStruct((TOP_K, n), jnp.int32),
                   jax.ShapeDtypeStruct((TOP_K, n), F32),
                   jax.ShapeDtypeStruct((TOP_K, n), jnp.int32),
                   jax.ShapeDtypeStruct((N_EXPERTS, 1), jnp.int32)],
        scratch_shapes=[pltpu.VMEM((N_EXPERTS, 1), F32),
                        pltpu.VMEM((2 * A_W, D_MODEL), BF16)],
        compiler_params=_cparams(("arbitrary",)),
        name="outproj_router",
    )(oa, ob, x2, wo, g2, rwh, rwl, rb)


def _row_copies_wait(like_ref, hbm_ref, sem):
    for _ in range(TOP_K):
        pltpu.make_async_copy(like_ref, hbm_ref.at[_tile_rows(0, T_ROWS)], sem).wait()


def _dispatch_kernel(zlo_ref, zhi_ref, dest_ref, h2_ref, xs_ref,
                     idx0_ref, idx1_ref, zero_ref, idx_sem, row_sem, zero_sem):
    i = pl.program_id(0)
    slot = i % 2
    idx_refs = (idx0_ref, idx1_ref)

    def idx_copy(tile, sl):
        return pltpu.make_async_copy(dest_ref.at[tile], idx_refs[sl], idx_sem.at[sl])

    @pl.when(i == 0)
    def _():
        idx_copy(0, 0).start()
        zero_ref[...] = jnp.zeros_like(zero_ref)

        def chunk_copy(c):
            return pltpu.make_async_copy(
                zero_ref, xs_ref.at[_tile_rows(c * ZERO_CHUNK, ZERO_CHUNK)], zero_sem)

        def issue_chunk(c, carry):
            chunk_copy(c).start()
            return carry

        def drain_chunk(c, carry):
            chunk_copy(c).wait()
            return carry

        def per_range(e, carry):
            lo = zlo_ref[e] // ZERO_CHUNK
            hi = zhi_ref[e] // ZERO_CHUNK
            lax.fori_loop(lo, hi, issue_chunk, 0)
            lax.fori_loop(lo, hi, drain_chunk, 0)
            return carry

        lax.fori_loop(0, N_EXPERTS + 1, per_range, 0)

    for sl in range(2):
        @pl.when(slot == sl)
        def _(sl=sl):
            @pl.when(i + 1 < pl.num_programs(0))
            def _():
                idx_copy(i + 1, 1 - sl).start()

            idx_copy(i, sl).wait()

            def body(r, carry):
                for k in range(TOP_K):
                    d = idx_refs[sl][k * T_ROWS + r]
                    pltpu.make_async_copy(h2_ref.at[_tile_rows(r, 1)], xs_ref.at[_tile_rows(d, 1)],
                                          row_sem).start()
                return carry

            lax.fori_loop(0, T_ROWS, body, 0)

    _row_copies_wait(h2_ref, xs_ref, row_sem)


def _dispatch(zero_lo, zero_hi, dest_tiles, h2, n_slots):
    n = h2.shape[0] // ROW_TILE
    return pl.pallas_call(
        _dispatch_kernel,
        grid_spec=pltpu.PrefetchScalarGridSpec(
            num_scalar_prefetch=2,
            grid=(n // T_ROWS,),
            in_specs=[pl.BlockSpec(memory_space=pl.ANY),
                      pl.BlockSpec((T_ROWS * ROW_TILE, LANES), lambda i, zl, zh: (i, 0))],
            out_specs=pl.BlockSpec(memory_space=pl.ANY),
            scratch_shapes=[pltpu.SMEM((TOP_K * T_ROWS,), jnp.int32),
                            pltpu.SMEM((TOP_K * T_ROWS,), jnp.int32),
                            pltpu.VMEM((ZERO_CHUNK * ROW_TILE, LANES), F32),
                            pltpu.SemaphoreType.DMA((2,)), pltpu.SemaphoreType.DMA,
                            pltpu.SemaphoreType.DMA]),
        out_shape=jax.ShapeDtypeStruct((n_slots * ROW_TILE, LANES), F32),
        compiler_params=_cparams(("arbitrary",)),
        name="dispatch",
    )(zero_lo, zero_hi, dest_tiles, h2)


def _moe_kernel(be_ref, nused_ref, xs_ref, wg_ref, bg_ref, wu_ref, bu_ref, wd_ref, bd_ref,
                ys_ref, wg_bf, wu_bf, wd_bf):
    i = pl.program_id(0)
    used = i < nused_ref[0]
    prev = be_ref[jnp.maximum(i - 1, 0)]

    @pl.when(used & ((i == 0) | (be_ref[i] != prev)))
    def _():
        wg_bf[...] = wg_ref[0].astype(BF16)
        wu_bf[...] = wu_ref[0].astype(BF16)
        wd_bf[...] = wd_ref[0].astype(BF16)

    @pl.when(jnp.logical_not(used))
    def _():
        ys_ref[...] = jnp.zeros_like(ys_ref)

    @pl.when(used)
    def _():
        x = _load_row_tiled(xs_ref).astype(BF16)
        g = jnp.dot(x, wg_bf[...], preferred_element_type=F32) + bg_ref[0]
        u = jnp.dot(x, wu_bf[...], preferred_element_type=F32) + bu_ref[0]
        g = jnp.minimum(g, SWIGLU_LIMIT)
        u = jnp.clip(u, -SWIGLU_LIMIT, SWIGLU_LIMIT)
        mid = (u + 1.0) * (g * (1.0 / (1.0 + jnp.exp(-SWIGLU_ALPHA * g))))
        _store_row_tiled(ys_ref, jnp.dot(mid.astype(BF16), wd_bf[...],
                                         preferred_element_type=F32) + bd_ref[0])


def _moe(block_expert, n_used, xs, wg, bg, wu, bu, wd, bd):
    n_slots = xs.shape[0] // ROW_TILE
    blk = lambda i, be, nu: (jnp.minimum(i, nu[0] - 1), 0)
    wsel = lambda i, be, nu: (be[jnp.minimum(i, nu[0] - 1)], 0, 0)
    wspec = pl.BlockSpec((1, D_MODEL, D_EXPERT), wsel)
    bspec = pl.BlockSpec((1, 1, D_EXPERT), wsel)
    return pl.pallas_call(
        _moe_kernel,
        grid_spec=pltpu.PrefetchScalarGridSpec(
            num_scalar_prefetch=2,
            grid=(n_slots // TM_MOE,),
            in_specs=[pl.BlockSpec((TM_MOE * ROW_TILE, LANES), blk),
                      wspec, bspec, wspec, bspec, wspec, bspec],
            out_specs=pl.BlockSpec((TM_MOE * ROW_TILE, LANES), lambda i, be, nu: (i, 0)),
            scratch_shapes=[pltpu.VMEM((D_MODEL, D_EXPERT), BF16)] * 3),
        out_shape=jax.ShapeDtypeStruct((n_slots * ROW_TILE, LANES), F32),
        compiler_params=_cparams(("arbitrary",)),
        name="moe_experts",
    )(block_expert, n_used, xs, wg, bg, wu, bu, wd, bd)


def _combine_kernel(dest_ref, x1_ref, gate_ref, ys_ref, out_ref, idx0_ref, idx1_ref, buf_ref,
                    idx_sem, row_sem):
    i = pl.program_id(0)
    n_tiles = pl.num_programs(0)
    slot = i % 2
    idx_refs = (idx0_ref, idx1_ref)

    def idx_copy(tile, sl):
        return pltpu.make_async_copy(dest_ref.at[tile], idx_refs[sl], idx_sem.at[sl])

    def gather_rows(sl):
        def body(r, carry):
            for k in range(TOP_K):
                d = idx_refs[sl][k * T_ROWS + r]
                pltpu.make_async_copy(ys_ref.at[_tile_rows(d, 1)],
                                      buf_ref.at[sl, k, _tile_rows(r, 1)],
                                      row_sem.at[sl]).start()
            return carry

        lax.fori_loop(0, T_ROWS, body, 0)

    @pl.when(i == 0)
    def _():
        idx_copy(0, 0).start()
        idx_copy(0, 0).wait()
        gather_rows(0)

        @pl.when(n_tiles > 1)
        def _():
            idx_copy(1, 1).start()

    for sl in range(2):
        @pl.when(slot == sl)
        def _(sl=sl):
            @pl.when(i + 1 < n_tiles)
            def _():
                idx_copy(i + 1, 1 - sl).wait()
                gather_rows(1 - sl)

            @pl.when(i + 2 < n_tiles)
            def _():
                idx_copy(i + 2, sl).start()

            _row_copies_wait(buf_ref.at[sl, 0], ys_ref, row_sem.at[sl])
            gate = gate_ref[...]
            acc = x1_ref[...]
            for k in range(TOP_K):
                acc = acc + gate[:, k:k + 1] * _load_row_tiled(buf_ref.at[sl, k])
            out_ref[...] = acc


def _combine(dest_tiles, x1, gates_t, ys):
    n = x1.shape[0]
    row = lambda i: (i, 0)
    return pl.pallas_call(
        _combine_kernel,
        grid=(n // T_ROWS,),
        in_specs=[pl.BlockSpec(memory_space=pl.ANY),
                  pl.BlockSpec((T_ROWS, D_MODEL), row),
                  pl.BlockSpec((T_ROWS, TOP_K), row),
                  pl.BlockSpec(memory_space=pl.ANY)],
        out_specs=pl.BlockSpec((T_ROWS, D_MODEL), row),
        out_shape=jax.ShapeDtypeStruct((n, D_MODEL), F32),
        scratch_shapes=[pltpu.SMEM((TOP_K * T_ROWS,), jnp.int32),
                            pltpu.SMEM((TOP_K * T_ROWS,), jnp.int32),
                        pltpu.VMEM((2, TOP_K, T_ROWS * ROW_TILE, LANES), F32),
                        pltpu.SemaphoreType.DMA((2,)), pltpu.SemaphoreType.DMA((2,))],
        compiler_params=_cparams(("arbitrary",)),
        name="combine",
    )(dest_tiles, x1, gates_t, ys)


def _rope_tables(seq):
    def freqs(dim):
        return ROPE_THETA ** (-(jnp.arange(0, dim, 2, dtype=F32) / dim))

    def table(ang):
        cos = jnp.cos(ang)
        sin = jnp.sin(ang)
        cos64 = jnp.concatenate([cos, cos], axis=-1)
        sin64 = jnp.concatenate([-sin, sin], axis=-1)
        return jnp.tile(cos64, (1, 2)), jnp.tile(sin64, (1, 2))

    t = jnp.arange(seq, dtype=F32)
    ang_1d = t[:, None] * freqs(HEAD_DIM)[None, :]
    rows = seq // GRID_W
    rowp = jnp.repeat(jnp.arange(rows, dtype=F32), GRID_W)
    colp = jnp.tile(jnp.arange(GRID_W, dtype=F32), rows)
    f_ax = freqs(HEAD_DIM // 2)
    ang_2d = jnp.concatenate([rowp[:, None] * f_ax[None, :], colp[:, None] * f_ax[None, :]],
                             axis=-1)
    return table(ang_1d) + table(ang_2d)


def _tile2(g):
    return jnp.tile(g.astype(F32), 2)[None, :]


def kernel(x, norm1_g, w_in, a_qnorm_g, a_knorm_g, a_lq1, a_lk1, a_lq2, a_lk2, a_subln_g,
           b_qnorm_g, b_knorm_g, w_out, norm2_g, router_w, router_b,
           w_gate, b_gate, w_up, b_up, w_down, b_down):
    b, s, d = x.shape
    n = b * s
    layer = 0
    x2 = x.reshape(n, d)
    cosa, sina, cosb, sinb = _rope_tables(s)

    qa, kat, va, qb, kbt, vb = _inproj(
        x2, norm1_g[layer][None, :], w_in[layer],
        _tile2(a_qnorm_g[layer]), _tile2(a_knorm_g[layer]),
        _tile2(b_qnorm_g[layer]), _tile2(b_knorm_g[layer]),
        cosa, sina, cosb, sinb, b, s)

    oa = _attn_a(a_lq1[layer][None, :], a_lk1[layer][None, :], a_lq2[layer][None, :],
                 a_lk2[layer][None, :], a_subln_g[layer][None, :],
                 qa.reshape(b, s, A_W), kat, va.reshape(b, s, A_W))
    ob = _attn_b(qb.reshape(b, s, A_W), kbt, vb.reshape(b, s, B_KV_W))

    rwt = router_w[layer].T.astype(F32)
    rwh = rwt.astype(BF16)
    rwl = (rwt - rwh.astype(F32)).astype(BF16)
    x1, h2, topi, gates, rank, counts = _outproj(
        oa.reshape(n, A_W), ob.reshape(n, A_W), x2, w_out[layer],
        norm2_g[layer][None, :], rwh, rwl, router_b[layer].astype(F32)[:, None])

    counts = counts[:, 0]
    padded = ((counts + TM_MOE - 1) // TM_MOE) * TM_MOE
    pad_end = jnp.cumsum(padded)
    pad_start = pad_end - padded
    eids = jnp.arange(N_EXPERTS, dtype=jnp.int32)
    start_of = jnp.sum(jnp.where(topi[:, :, None] == eids, pad_start, 0), axis=-1)
    dest = start_of + rank
    n_tiles = n // T_ROWS
    dest_tiles = dest.reshape(TOP_K, n_tiles, T_ROWS).transpose(1, 0, 2).reshape(
        n_tiles, TOP_K * T_ROWS)
    n_slots = n * TOP_K + N_EXPERTS * TM_MOE
    n_blocks = n_slots // TM_MOE
    block_start = jnp.arange(n_blocks, dtype=jnp.int32) * TM_MOE
    block_expert = jnp.minimum(
        jnp.sum((block_start[:, None] >= pad_end[None, :]).astype(jnp.int32), axis=1),
        N_EXPERTS - 1)
    n_used = (pad_end[-1:] // TM_MOE).astype(jnp.int32)

    zero_lo = jnp.concatenate([jnp.maximum(pad_end - TM_MOE, pad_start), pad_end[-1:]])
    zero_hi = jnp.concatenate([pad_end, jnp.full((1,), n_slots, jnp.int32)])
    xs = _dispatch(zero_lo.astype(jnp.int32), zero_hi.astype(jnp.int32), dest_tiles, h2,
                   n_slots)
    ys = _moe(block_expert, n_used, xs,
              w_gate[layer], b_gate[layer][:, None, :],
              w_up[layer], b_up[layer][:, None, :],
              w_down[layer], b_down[layer][:, None, :])
    out = _combine(dest_tiles, x1, gates.T, ys)
    return out.reshape(b, s, d)
```

```python
import math

import jax
import jax.numpy as jnp
from jax import lax
from jax.experimental import pallas as pl
from jax.experimental.pallas import tpu as pltpu

F32 = jnp.float32
BF16 = jnp.bfloat16

D_MODEL = 1024
HEAD_DIM = 64
A_HEADS = 4
B_HEADS = 8
B_KV_HEADS = 2
B_GROUP = B_HEADS // B_KV_HEADS
A_W = 512
B_KV_W = 128
D_IN_PROJ = 4 * A_W + 2 * B_KV_W
GRID_W = 64
ROPE_THETA = 10000.0
EPS = 1e-6
N_EXPERTS = 32
TOP_K = 4
D_EXPERT = 1024
SWIGLU_LIMIT = 7.0
SWIGLU_ALPHA = 1.702
LAM_INIT = 0.8 - 0.6 * math.exp(-0.3 * 0)

LANES = 128
ROW_TILE = D_MODEL // LANES
VMEM_LIMIT = 56 * 1024 * 1024

TM_PROJ = 512
TQ = 256
KC = 512
A_HEADS_PER_STEP = 4
TM_MOE = 512
T_ROWS = 512
ZERO_CHUNK = 64

def _cparams(sem):
    return pltpu.CompilerParams(dimension_semantics=sem, vmem_limit_bytes=VMEM_LIMIT)


def _head_norm_rope(p, gain, cos, sin, scale):
    tm, w = p.shape
    cw = min(w, 256)
    rep = cw // LANES
    r = lax.broadcasted_iota(jnp.int32, (cw, cw), 0) // HEAD_DIM
    c = lax.broadcasted_iota(jnp.int32, (cw, cw), 1) // HEAD_DIM
    ones_blk = jnp.where(r == c, 1.0, 0.0).astype(BF16)
    lane = lax.broadcasted_iota(jnp.int32, (tm, cw), 1)
    first_half = (lane % HEAD_DIM) < (HEAD_DIM // 2)
    gain_t = jnp.tile(gain, (1, rep))
    cos_t = jnp.tile(cos, (1, rep))
    sin_t = jnp.tile(sin, (1, rep))
    outs = []
    for c0 in range(0, w, cw):
        pc = p[:, c0:c0 + cw]
        ssum = jnp.dot((pc * pc).astype(BF16), ones_blk, preferred_element_type=F32)
        xn = pc * lax.rsqrt(ssum * (1.0 / HEAD_DIM) + EPS) * gain_t
        partner = jnp.where(first_half, pltpu.roll(xn, cw - HEAD_DIM // 2, 1),
                            pltpu.roll(xn, HEAD_DIM // 2, 1))
        outs.append((xn * cos_t + partner * sin_t) * scale)
    return outs, cw


def _inproj_kernel(x_ref, g1_ref, w_ref, gqa_ref, gka_ref, gqb_ref, gkb_ref,
                   cosa_ref, sina_ref, cosb_ref, sinb_ref,
                   qa_ref, kat_ref, va_ref, qb_ref, kbt_ref, vb_ref, wbf_ref):
    @pl.when(pl.program_id(0) == 0)
    def _():
        wbf_ref[...] = w_ref[...].astype(BF16)

    x = x_ref[...]
    ms = jnp.mean(x * x, axis=-1, keepdims=True)
    h = (x * lax.rsqrt(ms + EPS) * g1_ref[...]).astype(BF16)
    proj = jnp.dot(h, wbf_ref[...], preferred_element_type=F32)
    q_scale = HEAD_DIM ** -0.5 * math.log2(math.e)
    o = 0
    for ref, gain, cos, sin, scale, w, transposed in (
            (qa_ref, gqa_ref, cosa_ref, sina_ref, q_scale, A_W, False),
            (kat_ref, gka_ref, cosa_ref, sina_ref, 1.0, A_W, True),
            (va_ref, None, None, None, None, A_W, False),
            (qb_ref, gqb_ref, cosb_ref, sinb_ref, q_scale, A_W, False),
            (kbt_ref, gkb_ref, cosb_ref, sinb_ref, 1.0, B_KV_W, True),
            (vb_ref, None, None, None, None, B_KV_W, False)):
        p = proj[:, o:o + w]
        if gain is None:
            ref[...] = p.astype(BF16)
        else:
            outs, cw = _head_norm_rope(p, gain[...], cos[...], sin[...], scale)
            for j, y in enumerate(outs):
                if transposed:
                    ref[0, j * cw:(j + 1) * cw, :] = y.T.astype(BF16)
                else:
                    ref[:, j * cw:(j + 1) * cw] = y.astype(BF16)
        o += w


def _inproj(x2, g1, w_in, gqa, gka, gqb, gkb, cosa, sina, cosb, sinb, batch, seq):
    n = x2.shape[0]
    tm = TM_PROJ
    per_seq = seq // tm
    row = lambda i: (i, 0)
    const = lambda i: (0, 0)
    tab = lambda i: (i % per_seq, 0)
    ktr = lambda i: (i // per_seq, 0, i % per_seq)
    return pl.pallas_call(
        _inproj_kernel,
        grid=(n // tm,),
        in_specs=[pl.BlockSpec((tm, D_MODEL), row),
                  pl.BlockSpec((1, D_MODEL), const),
                  pl.BlockSpec((D_MODEL, D_IN_PROJ), const)]
                 + [pl.BlockSpec((1, LANES), const)] * 4
                 + [pl.BlockSpec((tm, LANES), tab)] * 4,
        out_specs=[pl.BlockSpec((tm, A_W), row),
                   pl.BlockSpec((1, A_W, tm), ktr),
                   pl.BlockSpec((tm, A_W), row),
                   pl.BlockSpec((tm, A_W), row),
                   pl.BlockSpec((1, B_KV_W, tm), ktr),
                   pl.BlockSpec((tm, B_KV_W), row)],
        out_shape=[jax.ShapeDtypeStruct((n, A_W), BF16),
                   jax.ShapeDtypeStruct((batch, A_W, seq), BF16),
                   jax.ShapeDtypeStruct((n, A_W), BF16),
                   jax.ShapeDtypeStruct((n, A_W), BF16),
                   jax.ShapeDtypeStruct((batch, B_KV_W, seq), BF16),
                   jax.ShapeDtypeStruct((n, B_KV_W), BF16)],
        scratch_shapes=[pltpu.VMEM((D_MODEL, D_IN_PROJ), BF16)],
        compiler_params=_cparams(("arbitrary",)),
        name="inproj",
    )(x2, g1, w_in, gqa, gka, gqb, gkb, cosa, sina, cosb, sinb)


def _softmax_pv_pipeline(queries, key_chunk, value_chunk, s_ref, seq):
    tq = queries[0].shape[0]
    n_chunks = seq // KC
    lanes_per_chunk = KC // LANES

    def qk_chunk(n, c, run_max):
        s = jnp.dot(queries[n], key_chunk(n, c), preferred_element_type=F32)
        s_ref[n % 2, :, c * KC:(c + 1) * KC] = s
        for k in range(lanes_per_chunk):
            run_max = jnp.maximum(run_max, s[:, k * LANES:(k + 1) * LANES])
        return run_max

    def pv_chunk(n, c, m, run_sum, acc):
        parts = []
        for k in range(lanes_per_chunk):
            lo = c * KC + k * LANES
            e = jnp.exp2(s_ref[n % 2, :, lo:lo + LANES] - m)
            run_sum = run_sum + e
            parts.append(e.astype(BF16))
        d = jnp.dot(jnp.concatenate(parts, axis=-1), value_chunk(n, c),
                    preferred_element_type=F32)
        return run_sum, d if acc is None else acc + d

    neg = jnp.full((tq, LANES), -jnp.inf, F32)
    run_max = neg
    for c in range(n_chunks):
        run_max = qk_chunk(0, c, run_max)
    results = []
    for n in range(len(queries)):
        m = jnp.broadcast_to(jnp.max(run_max, axis=-1, keepdims=True), (tq, LANES))
        run_sum = jnp.zeros((tq, LANES), F32)
        acc = None
        run_max = neg
        for c in range(n_chunks):
            if n + 1 < len(queries):
                run_max = qk_chunk(n + 1, c, run_max)
            run_sum, acc = pv_chunk(n, c, m, run_sum, acc)
        results.append((acc, jnp.sum(run_sum, axis=-1, keepdims=True)))
    return results


def _attn_a_kernel(lq1_ref, lk1_ref, lq2_ref, lk2_ref, gsub_ref,
                   q_ref, kt_ref, v_ref, o_ref, s_ref):
    seq = v_ref.shape[1]
    q = q_ref[0]
    n_mats = 2 * A_HEADS_PER_STEP
    queries = [q[:, n * HEAD_DIM:(n + 1) * HEAD_DIM] for n in range(n_mats)]
    results = _softmax_pv_pipeline(
        queries,
        lambda n, c: kt_ref[0, n * HEAD_DIM:(n + 1) * HEAD_DIM, c * KC:(c + 1) * KC],
        lambda n, c: v_ref[0, c * KC:(c + 1) * KC, (n // 2) * LANES:(n // 2 + 1) * LANES],
        s_ref, seq)
    lam = (jnp.exp(jnp.sum(lq1_ref[...] * lk1_ref[...], axis=-1, keepdims=True))
           - jnp.exp(jnp.sum(lq2_ref[...] * lk2_ref[...], axis=-1, keepdims=True))
           + LAM_INIT)
    for h in range(A_HEADS_PER_STEP):
        (o1, l1), (o2, l2) = results[2 * h], results[2 * h + 1]
        o = o1 / l1 - lam * (o2 / l2)
        ms = jnp.mean(o * o, axis=-1, keepdims=True)
        o = o * lax.rsqrt(ms + EPS) * gsub_ref[...] * (1.0 - LAM_INIT)
        o_ref[0, :, h * LANES:(h + 1) * LANES] = o.astype(BF16)


def _attn_a(lq1, lk1, lq2, lk2, gsub, qa, kat, va):
    b, s, _ = qa.shape
    w = A_HEADS_PER_STEP * 2 * HEAD_DIM
    vec = lambda width: pl.BlockSpec((1, width), lambda bi, h, i: (0, 0))
    return pl.pallas_call(
        _attn_a_kernel,
        grid=(b, A_HEADS // A_HEADS_PER_STEP, s // TQ),
        in_specs=[vec(HEAD_DIM)] * 4 + [vec(LANES)]
                 + [pl.BlockSpec((1, TQ, w), lambda bi, h, i: (bi, i, h)),
                    pl.BlockSpec((1, w, s), lambda bi, h, i: (bi, h, 0)),
                    pl.BlockSpec((1, s, w), lambda bi, h, i: (bi, 0, h))],
        out_specs=pl.BlockSpec((1, TQ, w), lambda bi, h, i: (bi, i, h)),
        out_shape=jax.ShapeDtypeStruct((b, s, A_W), BF16),
        scratch_shapes=[pltpu.VMEM((2, TQ, s), F32)],
        compiler_params=_cparams(("arbitrary",) * 3),
        name="attn_a",
    )(lq1, lk1, lq2, lk2, gsub, qa, kat, va)


def _attn_b_kernel(q_ref, kt_ref, v_ref, o_ref, s_ref):
    seq = v_ref.shape[1]
    q = q_ref[0]
    queries = [q[:, j * HEAD_DIM:(j + 1) * HEAD_DIM] for j in range(B_HEADS)]
    results = _softmax_pv_pipeline(
        queries,
        lambda n, c: kt_ref[0, (n // B_GROUP) * HEAD_DIM:(n // B_GROUP + 1) * HEAD_DIM,
                            c * KC:(c + 1) * KC],
        lambda n, c: v_ref[0, c * KC:(c + 1) * KC, :],
        s_ref, seq)
    low = lax.broadcasted_iota(jnp.int32, (q.shape[0], LANES), 1) < HEAD_DIM
    for pair in range(B_HEADS // 2):
        halves = []
        for side in range(2):
            j = 2 * pair + side
            acc, l = results[j]
            r = acc / l
            halves.append(r if j // B_GROUP == side else pltpu.roll(r, HEAD_DIM, 1))
        o_ref[0, :, pair * LANES:(pair + 1) * LANES] = (
            jnp.where(low, halves[0], halves[1]).astype(BF16))


def _attn_b(qb, kbt, vb):
    b, s, _ = qb.shape
    return pl.pallas_call(
        _attn_b_kernel,
        grid=(b, s // TQ),
        in_specs=[pl.BlockSpec((1, TQ, A_W), lambda bi, i: (bi, i, 0)),
                  pl.BlockSpec((1, B_KV_W, s), lambda bi, i: (bi, 0, 0)),
                  pl.BlockSpec((1, s, B_KV_W), lambda bi, i: (bi, 0, 0))],
        out_specs=pl.BlockSpec((1, TQ, A_W), lambda bi, i: (bi, i, 0)),
        out_shape=jax.ShapeDtypeStruct((b, s, A_W), BF16),
        scratch_shapes=[pltpu.VMEM((2, TQ, s), F32)],
        compiler_params=_cparams(("arbitrary",) * 2),
        name="attn_b",
    )(qb, kbt, vb)


def _store_row_tiled(ref, val):
    t = val.shape[0]
    for s in range(ROW_TILE):
        ref[pl.ds(s, t, stride=ROW_TILE), :] = val[:, s * LANES:(s + 1) * LANES]


def _load_row_tiled(ref):
    t = ref.shape[0] // ROW_TILE
    return jnp.concatenate([ref[pl.ds(s, t, stride=ROW_TILE), :] for s in range(ROW_TILE)],
                           axis=1)


def _tile_rows(row, count):
    return pl.ds(pl.multiple_of(row * ROW_TILE, ROW_TILE), count * ROW_TILE)


def _outproj_kernel(oa_ref, ob_ref, x_ref, wo_ref, g2_ref, rwh_ref, rwl_ref, rb_ref,
                    x1_ref, h2_ref, topi_ref, gate_ref, rank_ref, cnt_ref,
                    carry_ref, wbf_ref):
    i = pl.program_id(0)

    @pl.when(i == 0)
    def _():
        carry_ref[...] = jnp.zeros_like(carry_ref)
        wbf_ref[...] = wo_ref[...].astype(BF16)

    tm = x_ref.shape[0]
    a = (jnp.dot(oa_ref[...], wbf_ref[:A_W, :], preferred_element_type=F32)
         + jnp.dot(ob_ref[...], wbf_ref[A_W:, :], preferred_element_type=F32))
    x1 = x_ref[...] + a
    x1_ref[...] = x1
    ms = jnp.mean(x1 * x1, axis=-1, keepdims=True)
    h2 = x1 * lax.rsqrt(ms + EPS) * g2_ref[...]
    _store_row_tiled(h2_ref, h2)

    hi = h2.astype(BF16)
    lo = (h2 - hi.astype(F32)).astype(BF16)
    nt = (((1,), (1,)), ((), ()))
    rwh = rwh_ref[...]
    logits = (lax.dot_general(rwh, hi, nt, preferred_element_type=F32)
              + lax.dot_general(rwh, lo, nt, preferred_element_type=F32)
              + lax.dot_general(rwl_ref[...], hi, nt, preferred_element_type=F32)
              + rb_ref[...])

    eio = lax.broadcasted_iota(jnp.int32, logits.shape, 0)
    work = logits
    vals, idxs = [], []
    for _ in range(TOP_K):
        mx = jnp.max(work, axis=0, keepdims=True)
        idx = jnp.min(jnp.where(work == mx, eio, N_EXPERTS), axis=0, keepdims=True)
        vals.append(mx)
        idxs.append(idx)
        work = jnp.where(eio == idx, -jnp.inf, work)
    exps = [jnp.exp(v - vals[0]) for v in vals]
    tot = exps[0] + exps[1] + exps[2] + exps[3]

    member = jnp.zeros(logits.shape, F32)
    for idx in idxs:
        member = member + jnp.where(eio == idx, 1.0, 0.0)
    r = lax.broadcasted_iota(jnp.int32, (tm, tm), 0)
    c = lax.broadcasted_iota(jnp.int32, (tm, tm), 1)
    upper = jnp.where(r < c, 1.0, 0.0).astype(BF16)
    pos = jnp.dot(member.astype(BF16), upper, preferred_element_type=F32) + carry_ref[...]
    for k in range(TOP_K):
        topi_ref[k:k + 1, :] = idxs[k]
        gate_ref[k:k + 1, :] = exps[k] / tot
        rank_ref[k:k + 1, :] = jnp.sum(jnp.where(eio == idxs[k], pos, 0.0), axis=0,
                                       keepdims=True).astype(jnp.int32)
    carry = carry_ref[...] + jnp.sum(member, axis=1, keepdims=True)
    carry_ref[...] = carry
    cnt_ref[...] = carry.astype(jnp.int32)


def _outproj(oa, ob, x2, wo, g2, rwh, rwl, rb):
    n = x2.shape[0]
    tm = TM_PROJ
    row = lambda i: (i, 0)
    col = lambda i: (0, i)
    const = lambda i: (0, 0)
    return pl.pallas_call(
        _outproj_kernel,
        grid=(n // tm,),
        in_specs=[pl.BlockSpec((tm, A_W), row), pl.BlockSpec((tm, A_W), row),
                  pl.BlockSpec((tm, D_MODEL), row),
                  pl.BlockSpec((2 * A_W, D_MODEL), const),
                  pl.BlockSpec((1, D_MODEL), const),
                  pl.BlockSpec((N_EXPERTS, D_MODEL), const),
                  pl.BlockSpec((N_EXPERTS, D_MODEL), const),
                  pl.BlockSpec((N_EXPERTS, 1), const)],
        out_specs=[pl.BlockSpec((tm, D_MODEL), row), pl.BlockSpec((tm * ROW_TILE, LANES), row),
                   pl.BlockSpec((TOP_K, tm), col), pl.BlockSpec((TOP_K, tm), col),
                   pl.BlockSpec((TOP_K, tm), col),
                   pl.BlockSpec((N_EXPERTS, 1), const)],
        out_shape=[jax.ShapeDtypeStruct((n, D_MODEL), F32),
                   jax.ShapeDtypeStruct((n * ROW_TILE, LANES), F32),
                   jax.ShapeDtypeStruct((TOP_K, n), jnp.int32),
                   jax.ShapeDtypeStruct((TOP_K, n), F32),
                   jax.ShapeDtypeStruct((TOP_K, n), jnp.int32),
                   jax.ShapeDtypeStruct((N_EXPERTS, 1), jnp.int32)],
        scratch_shapes=[pltpu.VMEM((N_EXPERTS, 1), F32),
                        pltpu.VMEM((2 * A_W, D_MODEL), BF16)],
        compiler_params=_cparams(("arbitrary",)),
        name="outproj_router",
    )(oa, ob, x2, wo, g2, rwh, rwl, rb)


def _row_copies_wait(like_ref, hbm_ref, sem):
    for _ in range(TOP_K):
        pltpu.make_async_copy(like_ref, hbm_ref.at[_tile_rows(0, T_ROWS)], sem).wait()


def _dispatch_kernel(zlo_ref, zhi_ref, dest_ref, h2_ref, xs_ref,
                     idx0_ref, idx1_ref, zero_ref, idx_sem, row_sem, zero_sem):
    i = pl.program_id(0)
    slot = i % 2
    idx_refs = (idx0_ref, idx1_ref)

    def idx_copy(tile, sl):
        return pltpu.make_async_copy(dest_ref.at[tile], idx_refs[sl], idx_sem.at[sl])

    @pl.when(i == 0)
    def _():
        idx_copy(0, 0).start()
        zero_ref[...] = jnp.zeros_like(zero_ref)

        def chunk_copy(c):
            return pltpu.make_async_copy(
                zero_ref, xs_ref.at[_tile_rows(c * ZERO_CHUNK, ZERO_CHUNK)], zero_sem)

        def issue_chunk(c, carry):
            chunk_copy(c).start()
            return carry

        def drain_chunk(c, carry):
            chunk_copy(c).wait()
            return carry

        def per_range(e, carry):
            lo = zlo_ref[e] // ZERO_CHUNK
            hi = zhi_ref[e] // ZERO_CHUNK
            lax.fori_loop(lo, hi, issue_chunk, 0)
            lax.fori_loop(lo, hi, drain_chunk, 0)
            return carry

        lax.fori_loop(0, N_EXPERTS + 1, per_range, 0)

    for sl in range(2):
        @pl.when(slot == sl)
        def _(sl=sl):
            @pl.when(i + 1 < pl.num_programs(0))
            def _():
                idx_copy(i + 1, 1 - sl).start()

            idx_copy(i, sl).wait()

            def body(r, carry):
                for k in range(TOP_K):
                    d = idx_refs[sl][k * T_ROWS + r]
                    pltpu.make_async_copy(h2_ref.at[_tile_rows(r, 1)], xs_ref.at[_tile_rows(d, 1)],
                                          row_sem).start()
                return carry

            lax.fori_loop(0, T_ROWS, body, 0)

    _row_copies_wait(h2_ref, xs_ref, row_sem)


def _dispatch(zero_lo, zero_hi, dest_tiles, h2, n_slots):
    n = h2.shape[0] // ROW_TILE
    return pl.pallas_call(
        _dispatch_kernel,
        grid_spec=pltpu.PrefetchScalarGridSpec(
            num_scalar_prefetch=2,
            grid=(n // T_ROWS,),
            in_specs=[pl.BlockSpec(memory_space=pl.ANY),
                      pl.BlockSpec((T_ROWS * ROW_TILE, LANES), lambda i, zl, zh: (i, 0))],
            out_specs=pl.BlockSpec(memory_space=pl.ANY),
            scratch_shapes=[pltpu.SMEM((TOP_K * T_ROWS,), jnp.int32),
                            pltpu.SMEM((TOP_K * T_ROWS,), jnp.int32),
                            pltpu.VMEM((ZERO_CHUNK * ROW_TILE, LANES), F32),
                            pltpu.SemaphoreType.DMA((2,)), pltpu.SemaphoreType.DMA,
                            pltpu.SemaphoreType.DMA]),
        out_shape=jax.ShapeDtypeStruct((n_slots * ROW_TILE, LANES), F32),
        compiler_params=_cparams(("arbitrary",)),
        name="dispatch",
    )(zero_lo, zero_hi, dest_tiles, h2)


def _moe_kernel(be_ref, nused_ref, xs_ref, wg_ref, bg_ref, wu_ref, bu_ref, wd_ref, bd_ref,
                ys_ref, wg_bf, wu_bf, wd_bf):
    i = pl.program_id(0)
    used = i < nused_ref[0]
    prev = be_ref[jnp.maximum(i - 1, 0)]

    @pl.when(used & ((i == 0) | (be_ref[i] != prev)))
    def _():
        wg_bf[...] = wg_ref[0].astype(BF16)
        wu_bf[...] = wu_ref[0].astype(BF16)
        wd_bf[...] = wd_ref[0].astype(BF16)

    @pl.when(jnp.logical_not(used))
    def _():
        ys_ref[...] = jnp.zeros_like(ys_ref)

    @pl.when(used)
    def _():
        x = _load_row_tiled(xs_ref).astype(BF16)
        g = jnp.dot(x, wg_bf[...], preferred_element_type=F32) + bg_ref[0]
        u = jnp.dot(x, wu_bf[...], preferred_element_type=F32) + bu_ref[0]
        g = jnp.minimum(g, SWIGLU_LIMIT)
        u = jnp.clip(u, -SWIGLU_LIMIT, SWIGLU_LIMIT)
        mid = (u + 1.0) * (g * (1.0 / (1.0 + jnp.exp(-SWIGLU_ALPHA * g))))
        _store_row_tiled(ys_ref, jnp.dot(mid.astype(BF16), wd_bf[...],
                                         preferred_element_type=F32) + bd_ref[0])


def _moe(block_expert, n_used, xs, wg, bg, wu, bu, wd, bd):
    n_slots = xs.shape[0] // ROW_TILE
    blk = lambda i, be, nu: (jnp.minimum(i, nu[0] - 1), 0)
    wsel = lambda i, be, nu: (be[jnp.minimum(i, nu[0] - 1)], 0, 0)
    wspec = pl.BlockSpec((1, D_MODEL, D_EXPERT), wsel)
    bspec = pl.BlockSpec((1, 1, D_EXPERT), wsel)
    return pl.pallas_call(
        _moe_kernel,
        grid_spec=pltpu.PrefetchScalarGridSpec(
            num_scalar_prefetch=2,
            grid=(n_slots // TM_MOE,),
            in_specs=[pl.BlockSpec((TM_MOE * ROW_TILE, LANES), blk),
                      wspec, bspec, wspec, bspec, wspec, bspec],
            out_specs=pl.BlockSpec((TM_MOE * ROW_TILE, LANES), lambda i, be, nu: (i, 0)),
            scratch_shapes=[pltpu.VMEM((D_MODEL, D_EXPERT), BF16)] * 3),
        out_shape=jax.ShapeDtypeStruct((n_slots * ROW_TILE, LANES), F32),
        compiler_params=_cparams(("arbitrary",)),
        name="moe_experts",
    )(block_expert, n_used, xs, wg, bg, wu, bu, wd, bd)


def _combine_kernel(dest_ref, x1_ref, gate_ref, ys_ref, out_ref, idx0_ref, idx1_ref, buf_ref,
                    idx_sem, row_sem):
    i = pl.program_id(0)
    n_tiles = pl.num_programs(0)
    slot = i % 2
    idx_refs = (idx0_ref, idx1_ref)

    def idx_copy(tile, sl):
        return pltpu.make_async_copy(dest_ref.at[tile], idx_refs[sl], idx_sem.at[sl])

    def gather_rows(sl):
        def body(r, carry):
            for k in range(TOP_K):
                d = idx_refs[sl][k * T_ROWS + r]
                pltpu.make_async_copy(ys_ref.at[_tile_rows(d, 1)],
                                      buf_ref.at[sl, k, _tile_rows(r, 1)],
                                      row_sem.at[sl]).start()
            return carry

        lax.fori_loop(0, T_ROWS, body, 0)

    @pl.when(i == 0)
    def _():
        idx_copy(0, 0).start()
        idx_copy(0, 0).wait()
        gather_rows(0)

        @pl.when(n_tiles > 1)
        def _():
            idx_copy(1, 1).start()

    for sl in range(2):
        @pl.when(slot == sl)
        def _(sl=sl):
            @pl.when(i + 1 < n_tiles)
            def _():
                idx_copy(i + 1, 1 - sl).wait()
                gather_rows(1 - sl)

            @pl.when(i + 2 < n_tiles)
            def _():
                idx_copy(i + 2, sl).start()

            _row_copies_wait(buf_ref.at[sl, 0], ys_ref, row_sem.at[sl])
            gate = gate_ref[...]
            acc = x1_ref[...]
            for k in range(TOP_K):
                acc = acc + gate[:, k:k + 1] * _load_row_tiled(buf_ref.at[sl, k])
            out_ref[...] = acc


def _combine(dest_tiles, x1, gates_t, ys):
    n = x1.shape[0]
    row = lambda i: (i, 0)
    return pl.pallas_call(
        _combine_kernel,
        grid=(n // T_ROWS,),
        in_specs=[pl.BlockSpec(memory_space=pl.ANY),
                  pl.BlockSpec((T_ROWS, D_MODEL), row),
                  pl.BlockSpec((T_ROWS, TOP_K), row),
                  pl.BlockSpec(memory_space=pl.ANY)],
        out_specs=pl.BlockSpec((T_ROWS, D_MODEL), row),
        out_shape=jax.ShapeDtypeStruct((n, D_MODEL), F32),
        scratch_shapes=[pltpu.SMEM((TOP_K * T_ROWS,), jnp.int32),
                            pltpu.SMEM((TOP_K * T_ROWS,), jnp.int32),
                        pltpu.VMEM((2, TOP_K, T_ROWS * ROW_TILE, LANES), F32),
                        pltpu.SemaphoreType.DMA((2,)), pltpu.SemaphoreType.DMA((2,))],
        compiler_params=_cparams(("arbitrary",)),
        name="combine",
    )(dest_tiles, x1, gates_t, ys)


def _rope_tables(seq):
    def freqs(dim):
        return ROPE_THETA ** (-(jnp.arange(0, dim, 2, dtype=F32) / dim))

    def table(ang):
        cos = jnp.cos(ang)
        sin = jnp.sin(ang)
        cos64 = jnp.concatenate([cos, cos], axis=-1)
        sin64 = jnp.concatenate([-sin, sin], axis=-1)
        return jnp.tile(cos64, (1, 2)), jnp.tile(sin64, (1, 2))

    t = jnp.arange(seq, dtype=F32)
    ang_1d = t[:, None] * freqs(HEAD_DIM)[None, :]
    rows = seq // GRID_W
    rowp = jnp.repeat(jnp.arange(rows, dtype=F32), GRID_W)
    colp = jnp.tile(jnp.arange(GRID_W, dtype=F32), rows)
    f_ax = freqs(HEAD_DIM // 2)
    ang_2d = jnp.concatenate([rowp[:, None] * f_ax[None, :], colp[:, None] * f_ax[None, :]],
                             axis=-1)
    return table(ang_1d) + table(ang_2d)


def _tile2(g):
    return jnp.tile(g.astype(F32), 2)[None, :]


def kernel(x, norm1_g, w_in, a_qnorm_g, a_knorm_g, a_lq1, a_lk1, a_lq2, a_lk2, a_subln_g,
           b_qnorm_g, b_knorm_g, w_out, norm2_g, router_w, router_b,
           w_gate, b_gate, w_up, b_up, w_down, b_down):
    b, s, d = x.shape
    n = b * s
    layer = 0
    x2 = x.reshape(n, d)
    cosa, sina, cosb, sinb = _rope_tables(s)

    qa, kat, va, qb, kbt, vb = _inproj(
        x2, norm1_g[layer][None, :], w_in[layer],
        _tile2(a_qnorm_g[layer]), _tile2(a_knorm_g[layer]),
        _tile2(b_qnorm_g[layer]), _tile2(b_knorm_g[layer]),
        cosa, sina, cosb, sinb, b, s)

    oa = _attn_a(a_lq1[layer][None, :], a_lk1[layer][None, :], a_lq2[layer][None, :],
                 a_lk2[layer][None, :], a_subln_g[layer][None, :],
                 qa.reshape(b, s, A_W), kat, va.reshape(b, s, A_W))
    ob = _attn_b(qb.reshape(b, s, A_W), kbt, vb.reshape(b, s, B_KV_W))

    rwt = router_w[layer].T.astype(F32)
    rwh = rwt.astype(BF16)
    rwl = (rwt - rwh.astype(F32)).astype(BF16)
    x1, h2, topi, gates, rank, counts = _outproj(
        oa.reshape(n, A_W), ob.reshape(n, A_W), x2, w_out[layer],
        norm2_g[layer][None, :], rwh, rwl, router_b[layer].astype(F32)[:, None])

    counts = counts[:, 0]
    padded = ((counts + TM_MOE - 1) // TM_MOE) * TM_MOE
    pad_end = jnp.cumsum(padded)
    pad_start = pad_end - padded
    eids = jnp.arange(N_EXPERTS, dtype=jnp.int32)
    start_of = jnp.sum(jnp.where(topi[:, :, None] == eids, pad_start, 0), axis=-1)
    dest = start_of + rank
    n_tiles = n // T_ROWS
    dest_tiles = dest.reshape(TOP_K, n_tiles, T_ROWS).transpose(1, 0, 2).reshape(
        n_tiles, TOP_K * T_ROWS)
    n_slots = n * TOP_K + N_EXPERTS * TM_MOE
    n_blocks = n_slots // TM_MOE
    block_start = jnp.arange(n_blocks, dtype=jnp.int32) * TM_MOE
    block_expert = jnp.minimum(
        jnp.sum((block_start[:, None] >= pad_end[None, :]).astype(jnp.int32), axis=1),
        N_EXPERTS - 1)
    n_used = (pad_end[-1:] // TM_MOE).astype(jnp.int32)

    zero_lo = jnp.concatenate([jnp.maximum(pad_end - TM_MOE, pad_start), pad_end[-1:]])
    zero_hi = jnp.concatenate([pad_end, jnp.full((1,), n_slots, jnp.int32)])
    xs = _dispatch(zero_lo.astype(jnp.int32), zero_hi.astype(jnp.int32), dest_tiles, h2,
                   n_slots)
    ys = _moe(block_expert, n_used, xs,
              w_gate[layer], b_gate[layer][:, None, :],
              w_up[layer], b_up[layer][:, None, :],
              w_down[layer], b_down[layer][:, None, :])
    out = _combine(dest_tiles, x1, gates.T, ys)
    return out.reshape(b, s, d)
```

```python
import math

import jax
import jax.numpy as jnp
from jax import lax
from jax.experimental import pallas as pl
from jax.experimental.pallas import tpu as pltpu

F32 = jnp.float32
BF16 = jnp.bfloat16

D_MODEL = 1024
HEAD_DIM = 64
A_HEADS = 4
B_HEADS = 8
B_KV_HEADS = 2
B_GROUP = B_HEADS // B_KV_HEADS
A_W = 512
B_KV_W = 128
D_IN_PROJ = 4 * A_W + 2 * B_KV_W
GRID_W = 64
ROPE_THETA = 10000.0
EPS = 1e-6
N_EXPERTS = 32
TOP_K = 4
D_EXPERT = 1024
SWIGLU_LIMIT = 7.0
SWIGLU_ALPHA = 1.702
LAM_INIT = 0.8 - 0.6 * math.exp(-0.3 * 0)

LANES = 128
ROW_TILE = D_MODEL // LANES
VMEM_LIMIT = 56 * 1024 * 1024

TM_PROJ = 512
TQ = 256
KC = 512
A_HEADS_PER_STEP = 4
TM_MOE = 512
T_ROWS = 512
ZERO_CHUNK = 64

def _cparams(sem):
    return pltpu.CompilerParams(dimension_semantics=sem, vmem_limit_bytes=VMEM_LIMIT)


def _head_norm_rope(p, gain, cos, sin, scale):
    tm, w = p.shape
    cw = min(w, 256)
    rep = cw // LANES
    r = lax.broadcasted_iota(jnp.int32, (cw, cw), 0) // HEAD_DIM
    c = lax.broadcasted_iota(jnp.int32, (cw, cw), 1) // HEAD_DIM
    ones_blk = jnp.where(r == c, 1.0, 0.0).astype(BF16)
    lane = lax.broadcasted_iota(jnp.int32, (tm, cw), 1)
    first_half = (lane % HEAD_DIM) < (HEAD_DIM // 2)
    gain_t = jnp.tile(gain, (1, rep))
    cos_t = jnp.tile(cos, (1, rep))
    sin_t = jnp.tile(sin, (1, rep))
    outs = []
    for c0 in range(0, w, cw):
        pc = p[:, c0:c0 + cw]
        ssum = jnp.dot((pc * pc).astype(BF16), ones_blk, preferred_element_type=F32)
        xn = pc * lax.rsqrt(ssum * (1.0 / HEAD_DIM) + EPS) * gain_t
        partner = jnp.where(first_half, pltpu.roll(xn, cw - HEAD_DIM // 2, 1),
                            pltpu.roll(xn, HEAD_DIM // 2, 1))
        outs.append((xn * cos_t + partner * sin_t) * scale)
    return outs, cw


def _inproj_kernel(x_ref, g1_ref, w_ref, gqa_ref, gka_ref, gqb_ref, gkb_ref,
                   cosa_ref, sina_ref, cosb_ref, sinb_ref,
                   qa_ref, kat_ref, va_ref, qb_ref, kbt_ref, vb_ref, wbf_ref):
    @pl.when(pl.program_id(0) == 0)
    def _():
        wbf_ref[...] = w_ref[...].astype(BF16)

    x = x_ref[...]
    ms = jnp.mean(x * x, axis=-1, keepdims=True)
    h = (x * lax.rsqrt(ms + EPS) * g1_ref[...]).astype(BF16)
    proj = jnp.dot(h, wbf_ref[...], preferred_element_type=F32)
    q_scale = HEAD_DIM ** -0.5 * math.log2(math.e)
    o = 0
    for ref, gain, cos, sin, scale, w, transposed in (
            (qa_ref, gqa_ref, cosa_ref, sina_ref, q_scale, A_W, False),
            (kat_ref, gka_ref, cosa_ref, sina_ref, 1.0, A_W, True),
            (va_ref, None, None, None, None, A_W, False),
            (qb_ref, gqb_ref, cosb_ref, sinb_ref, q_scale, A_W, False),
            (kbt_ref, gkb_ref, cosb_ref, sinb_ref, 1.0, B_KV_W, True),
            (vb_ref, None, None, None, None, B_KV_W, False)):
        p = proj[:, o:o + w]
        if gain is None:
            ref[...] = p.astype(BF16)
        else:
            outs, cw = _head_norm_rope(p, gain[...], cos[...], sin[...], scale)
            for j, y in enumerate(outs):
                if transposed:
                    ref[0, j * cw:(j + 1) * cw, :] = y.T.astype(BF16)
                else:
                    ref[:, j * cw:(j + 1) * cw] = y.astype(BF16)
        o += w


def _inproj(x2, g1, w_in, gqa, gka, gqb, gkb, cosa, sina, cosb, sinb, batch, seq):
    n = x2.shape[0]
    tm = TM_PROJ
    per_seq = seq // tm
    row = lambda i: (i, 0)
    const = lambda i: (0, 0)
    tab = lambda i: (i % per_seq, 0)
    ktr = lambda i: (i // per_seq, 0, i % per_seq)
    return pl.pallas_call(
        _inproj_kernel,
        grid=(n // tm,),
        in_specs=[pl.BlockSpec((tm, D_MODEL), row),
                  pl.BlockSpec((1, D_MODEL), const),
                  pl.BlockSpec((D_MODEL, D_IN_PROJ), const)]
                 + [pl.BlockSpec((1, LANES), const)] * 4
                 + [pl.BlockSpec((tm, LANES), tab)] * 4,
        out_specs=[pl.BlockSpec((tm, A_W), row),
                   pl.BlockSpec((1, A_W, tm), ktr),
                   pl.BlockSpec((tm, A_W), row),
                   pl.BlockSpec((tm, A_W), row),
                   pl.BlockSpec((1, B_KV_W, tm), ktr),
                   pl.BlockSpec((tm, B_KV_W), row)],
        out_shape=[jax.ShapeDtypeStruct((n, A_W), BF16),
                   jax.ShapeDtypeStruct((batch, A_W, seq), BF16),
                   jax.ShapeDtypeStruct((n, A_W), BF16),
                   jax.ShapeDtypeStruct((n, A_W), BF16),
                   jax.ShapeDtypeStruct((batch, B_KV_W, seq), BF16),
                   jax.ShapeDtypeStruct((n, B_KV_W), BF16)],
        scratch_shapes=[pltpu.VMEM((D_MODEL, D_IN_PROJ), BF16)],
        compiler_params=_cparams(("arbitrary",)),
        name="inproj",
    )(x2, g1, w_in, gqa, gka, gqb, gkb, cosa, sina, cosb, sinb)


def _softmax_pv_pipeline(queries, key_chunk, value_chunk, s_ref, seq):
    tq = queries[0].shape[0]
    n_chunks = seq // KC
    lanes_per_chunk = KC // LANES

    def qk_chunk(n, c, run_max):
        s = jnp.dot(queries[n], key_chunk(n, c), preferred_element_type=F32)
        s_ref[n % 2, :, c * KC:(c + 1) * KC] = s
        for k in range(lanes_per_chunk):
            run_max = jnp.maximum(run_max, s[:, k * LANES:(k + 1) * LANES])
        return run_max

    def pv_chunk(n, c, m, run_sum, acc):
        parts = []
        for k in range(lanes_per_chunk):
            lo = c * KC + k * LANES
            e = jnp.exp2(s_ref[n % 2, :, lo:lo + LANES] - m)
            run_sum = run_sum + e
            parts.append(e.astype(BF16))
        d = jnp.dot(jnp.concatenate(parts, axis=-1), value_chunk(n, c),
                    preferred_element_type=F32)
        return run_sum, d if acc is None else acc + d

    neg = jnp.full((tq, LANES), -jnp.inf, F32)
    run_max = neg
    for c in range(n_chunks):
        run_max = qk_chunk(0, c, run_max)
    results = []
    for n in range(len(queries)):
        m = jnp.broadcast_to(jnp.max(run_max, axis=-1, keepdims=True), (tq, LANES))
        run_sum = jnp.zeros((tq, LANES), F32)
        acc = None
        run_max = neg
        for c in range(n_chunks):
            if n + 1 < len(queries):
                run_max = qk_chunk(n + 1, c, run_max)
            run_sum, acc = pv_chunk(n, c, m, run_sum, acc)
        results.append((acc, jnp.sum(run_sum, axis=-1, keepdims=True)))
    return results


def _attn_a_kernel(lq1_ref, lk1_ref, lq2_ref, lk2_ref, gsub_ref,
                   q_ref, kt_ref, v_ref, o_ref, s_ref):
    seq = v_ref.shape[1]
    q = q_ref[0]
    n_mats = 2 * A_HEADS_PER_STEP
    queries = [q[:, n * HEAD_DIM:(n + 1) * HEAD_DIM] for n in range(n_mats)]
    results = _softmax_pv_pipeline(
        queries,
        lambda n, c: kt_ref[0, n * HEAD_DIM:(n + 1) * HEAD_DIM, c * KC:(c + 1) * KC],
        lambda n, c: v_ref[0, c * KC:(c + 1) * KC, (n // 2) * LANES:(n // 2 + 1) * LANES],
        s_ref, seq)
    lam = (jnp.exp(jnp.sum(lq1_ref[...] * lk1_ref[...], axis=-1, keepdims=True))
           - jnp.exp(jnp.sum(lq2_ref[...] * lk2_ref[...], axis=-1, keepdims=True))
           + LAM_INIT)
    for h in range(A_HEADS_PER_STEP):
        (o1, l1), (o2, l2) = results[2 * h], results[2 * h + 1]
        o = o1 / l1 - lam * (o2 / l2)
        ms = jnp.mean(o * o, axis=-1, keepdims=True)
        o = o * lax.rsqrt(ms + EPS) * gsub_ref[...] * (1.0 - LAM_INIT)
        o_ref[0, :, h * LANES:(h + 1) * LANES] = o.astype(BF16)


def _attn_a(lq1, lk1, lq2, lk2, gsub, qa, kat, va):
    b, s, _ = qa.shape
    w = A_HEADS_PER_STEP * 2 * HEAD_DIM
    vec = lambda width: pl.BlockSpec((1, width), lambda bi, h, i: (0, 0))
    return pl.pallas_call(
        _attn_a_kernel,
        grid=(b, A_HEADS // A_HEADS_PER_STEP, s // TQ),
        in_specs=[vec(HEAD_DIM)] * 4 + [vec(LANES)]
                 + [pl.BlockSpec((1, TQ, w), lambda bi, h, i: (bi, i, h)),
                    pl.BlockSpec((1, w, s), lambda bi, h, i: (bi, h, 0)),
                    pl.BlockSpec((1, s, w), lambda bi, h, i: (bi, 0, h))],
        out_specs=pl.BlockSpec((1, TQ, w), lambda bi, h, i: (bi, i, h)),
        out_shape=jax.ShapeDtypeStruct((b, s, A_W), BF16),
        scratch_shapes=[pltpu.VMEM((2, TQ, s), F32)],
        compiler_params=_cparams(("arbitrary",) * 3),
        name="attn_a",
    )(lq1, lk1, lq2, lk2, gsub, qa, kat, va)


def _attn_b_kernel(q_ref, kt_ref, v_ref, o_ref, s_ref):
    seq = v_ref.shape[1]
    q = q_ref[0]
    queries = [q[:, j * HEAD_DIM:(j + 1) * HEAD_DIM] for j in range(B_HEADS)]
    results = _softmax_pv_pipeline(
        queries,
        lambda n, c: kt_ref[0, (n // B_GROUP) * HEAD_DIM:(n // B_GROUP + 1) * HEAD_DIM,
                            c * KC:(c + 1) * KC],
        lambda n, c: v_ref[0, c * KC:(c + 1) * KC, :],
        s_ref, seq)
    low = lax.broadcasted_iota(jnp.int32, (q.shape[0], LANES), 1) < HEAD_DIM
    for pair in range(B_HEADS // 2):
        halves = []
        for side in range(2):
            j = 2 * pair + side
            acc, l = results[j]
            r = acc / l
            halves.append(r if j // B_GROUP == side else pltpu.roll(r, HEAD_DIM, 1))
        o_ref[0, :, pair * LANES:(pair + 1) * LANES] = (
            jnp.where(low, halves[0], halves[1]).astype(BF16))


def _attn_b(qb, kbt, vb):
    b, s, _ = qb.shape
    return pl.pallas_call(
        _attn_b_kernel,
        grid=(b, s // TQ),
        in_specs=[pl.BlockSpec((1, TQ, A_W), lambda bi, i: (bi, i, 0)),
                  pl.BlockSpec((1, B_KV_W, s), lambda bi, i: (bi, 0, 0)),
                  pl.BlockSpec((1, s, B_KV_W), lambda bi, i: (bi, 0, 0))],
        out_specs=pl.BlockSpec((1, TQ, A_W), lambda bi, i: (bi, i, 0)),
        out_shape=jax.ShapeDtypeStruct((b, s, A_W), BF16),
        scratch_shapes=[pltpu.VMEM((2, TQ, s), F32)],
        compiler_params=_cparams(("arbitrary",) * 2),
        name="attn_b",
    )(qb, kbt, vb)


def _store_row_tiled(ref, val):
    t = val.shape[0]
    for s in range(ROW_TILE):
        ref[pl.ds(s, t, stride=ROW_TILE), :] = val[:, s * LANES:(s + 1) * LANES]


def _load_row_tiled(ref):
    t = ref.shape[0] // ROW_TILE
    return jnp.concatenate([ref[pl.ds(s, t, stride=ROW_TILE), :] for s in range(ROW_TILE)],
                           axis=1)


def _tile_rows(row, count):
    return pl.ds(pl.multiple_of(row * ROW_TILE, ROW_TILE), count * ROW_TILE)


def _outproj_kernel(oa_ref, ob_ref, x_ref, wo_ref, g2_ref, rwh_ref, rwl_ref, rb_ref,
                    x1_ref, h2_ref, topi_ref, gate_ref, rank_ref, cnt_ref,
                    carry_ref, wbf_ref):
    i = pl.program_id(0)

    @pl.when(i == 0)
    def _():
        carry_ref[...] = jnp.zeros_like(carry_ref)
        wbf_ref[...] = wo_ref[...].astype(BF16)

    tm = x_ref.shape[0]
    a = (jnp.dot(oa_ref[...], wbf_ref[:A_W, :], preferred_element_type=F32)
         + jnp.dot(ob_ref[...], wbf_ref[A_W:, :], preferred_element_type=F32))
    x1 = x_ref[...] + a
    x1_ref[...] = x1
    ms = jnp.mean(x1 * x1, axis=-1, keepdims=True)
    h2 = x1 * lax.rsqrt(ms + EPS) * g2_ref[...]
    _store_row_tiled(h2_ref, h2)

    hi = h2.astype(BF16)
    lo = (h2 - hi.astype(F32)).astype(BF16)
    nt = (((1,), (1,)), ((), ()))
    rwh = rwh_ref[...]
    logits = (lax.dot_general(rwh, hi, nt, preferred_element_type=F32)
              + lax.dot_general(rwh, lo, nt, preferred_element_type=F32)
              + lax.dot_general(rwl_ref[...], hi, nt, preferred_element_type=F32)
              + rb_ref[...])

    eio = lax.broadcasted_iota(jnp.int32, logits.shape, 0)
    work = logits
    vals, idxs = [], []
    for _ in range(TOP_K):
        mx = jnp.max(work, axis=0, keepdims=True)
        idx = jnp.min(jnp.where(work == mx, eio, N_EXPERTS), axis=0, keepdims=True)
        vals.append(mx)
        idxs.append(idx)
        work = jnp.where(eio == idx, -jnp.inf, work)
    exps = [jnp.exp(v - vals[0]) for v in vals]
    tot = exps[0] + exps[1] + exps[2] + exps[3]

    member = jnp.zeros(logits.shape, F32)
    for idx in idxs:
        member = member + jnp.where(eio == idx, 1.0, 0.0)
    r = lax.broadcasted_iota(jnp.int32, (tm, tm), 0)
    c = lax.broadcasted_iota(jnp.int32, (tm, tm), 1)
    upper = jnp.where(r < c, 1.0, 0.0).astype(BF16)
    pos = jnp.dot(member.astype(BF16), upper, preferred_element_type=F32) + carry_ref[...]
    for k in range(TOP_K):
        topi_ref[k:k + 1, :] = idxs[k]
        gate_ref[k:k + 1, :] = exps[k] / tot
        rank_ref[k:k + 1, :] = jnp.sum(jnp.where(eio == idxs[k], pos, 0.0), axis=0,
                                       keepdims=True).astype(jnp.int32)
    carry = carry_ref[...] + jnp.sum(member, axis=1, keepdims=True)
    carry_ref[...] = carry
    cnt_ref[...] = carry.astype(jnp.int32)


def _outproj(oa, ob, x2, wo, g2, rwh, rwl, rb):
    n = x2.shape[0]
    tm = TM_PROJ
    row = lambda i: (i, 0)
    col = lambda i: (0, i)
    const = lambda i: (0, 0)
    return pl.pallas_call(
        _outproj_kernel,
        grid=(n // tm,),
        in_specs=[pl.BlockSpec((tm, A_W), row), pl.BlockSpec((tm, A_W), row),
                  pl.BlockSpec((tm, D_MODEL), row),
                  pl.BlockSpec((2 * A_W, D_MODEL), const),
                  pl.BlockSpec((1, D_MODEL), const),
                  pl.BlockSpec((N_EXPERTS, D_MODEL), const),
                  pl.BlockSpec((N_EXPERTS, D_MODEL), const),
                  pl.BlockSpec((N_EXPERTS, 1), const)],
        out_specs=[pl.BlockSpec((tm, D_MODEL), row), pl.BlockSpec((tm * ROW_TILE, LANES), row),
                   pl.BlockSpec((TOP_K, tm), col), pl.BlockSpec((TOP_K, tm), col),
                   pl.BlockSpec((TOP_K, tm), col),
                   pl.BlockSpec((N_EXPERTS, 1), const)],
        out_shape=[jax.ShapeDtypeStruct((n, D_MODEL), F32),
                   jax.ShapeDtypeStruct((n * ROW_TILE, LANES), F32),
                   jax.ShapeDtypeStruct((TOP_K, n), jnp.int32),
                   jax.ShapeDtypeStruct((TOP_K, n), F32),
                   jax.ShapeDtypeStruct((TOP_K, n), jnp.int32),
                   jax.ShapeDtypeStruct((N_EXPERTS, 1), jnp.int32)],
        scratch_shapes=[pltpu.VMEM((N_EXPERTS, 1), F32),
                        pltpu.VMEM((2 * A_W, D_MODEL), BF16)],
        compiler_params=_cparams(("arbitrary",)),
        name="outproj_router",
    )(oa, ob, x2, wo, g2, rwh, rwl, rb)


def _row_copies_wait(like_ref, hbm_ref, sem):
    for _ in range(TOP_K):
        pltpu.make_async_copy(like_ref, hbm_ref.at[_tile_rows(0, T_ROWS)], sem).wait()


def _dispatch_kernel(zlo_ref, zhi_ref, dest_ref, h2_ref, xs_ref,
                     idx0_ref, idx1_ref, zero_ref, idx_sem, row_sem, zero_sem):
    i = pl.program_id(0)
    slot = i % 2
    idx_refs = (idx0_ref, idx1_ref)

    def idx_copy(tile, sl):
        return pltpu.make_async_copy(dest_ref.at[tile], idx_refs[sl], idx_sem.at[sl])

    @pl.when(i == 0)
    def _():
        idx_copy(0, 0).start()
        zero_ref[...] = jnp.zeros_like(zero_ref)

        def chunk_copy(c):
            return pltpu.make_async_copy(
                zero_ref, xs_ref.at[_tile_rows(c * ZERO_CHUNK, ZERO_CHUNK)], zero_sem)

        def issue_chunk(c, carry):
            chunk_copy(c).start()
            return carry

        def drain_chunk(c, carry):
            chunk_copy(c).wait()
            return carry

        def per_range(e, carry):
            lo = zlo_ref[e] // ZERO_CHUNK
            hi = zhi_ref[e] // ZERO_CHUNK
            lax.fori_loop(lo, hi, issue_chunk, 0)
            lax.fori_loop(lo, hi, drain_chunk, 0)
            return carry

        lax.fori_loop(0, N_EXPERTS + 1, per_range, 0)

    for sl in range(2):
        @pl.when(slot == sl)
        def _(sl=sl):
            @pl.when(i + 1 < pl.num_programs(0))
            def _():
                idx_copy(i + 1, 1 - sl).start()

            idx_copy(i, sl).wait()

            def body(r, carry):
                for k in range(TOP_K):
                    d = idx_refs[sl][k * T_ROWS + r]
                    pltpu.make_async_copy(h2_ref.at[_tile_rows(r, 1)], xs_ref.at[_tile_rows(d, 1)],
                                          row_sem).start()
                return carry

            lax.fori_loop(0, T_ROWS, body, 0)

    _row_copies_wait(h2_ref, xs_ref, row_sem)


def _dispatch(zero_lo, zero_hi, dest_tiles, h2, n_slots):
    n = h2.shape[0] // ROW_TILE
    return pl.pallas_call(
        _dispatch_kernel,
        grid_spec=pltpu.PrefetchScalarGridSpec(
            num_scalar_prefetch=2,
            grid=(n // T_ROWS,),
            in_specs=[pl.BlockSpec(memory_space=pl.ANY),
                      pl.BlockSpec((T_ROWS * ROW_TILE, LANES), lambda i, zl, zh: (i, 0))],
            out_specs=pl.BlockSpec(memory_space=pl.ANY),
            scratch_shapes=[pltpu.SMEM((TOP_K * T_ROWS,), jnp.int32),
                            pltpu.SMEM((TOP_K * T_ROWS,), jnp.int32),
                            pltpu.VMEM((ZERO_CHUNK * ROW_TILE, LANES), F32),
                            pltpu.SemaphoreType.DMA((2,)), pltpu.SemaphoreType.DMA,
                            pltpu.SemaphoreType.DMA]),
        out_shape=jax.ShapeDtypeStruct((n_slots * ROW_TILE, LANES), F32),
        compiler_params=_cparams(("arbitrary",)),
        name="dispatch",
    )(zero_lo, zero_hi, dest_tiles, h2)


def _moe_kernel(be_ref, nused_ref, xs_ref, wg_ref, bg_ref, wu_ref, bu_ref, wd_ref, bd_ref,
                ys_ref, wg_bf, wu_bf, wd_bf):
    i = pl.program_id(0)
    used = i < nused_ref[0]
    prev = be_ref[jnp.maximum(i - 1, 0)]

    @pl.when(used & ((i == 0) | (be_ref[i] != prev)))
    def _():
        wg_bf[...] = wg_ref[0].astype(BF16)
        wu_bf[...] = wu_ref[0].astype(BF16)
        wd_bf[...] = wd_ref[0].astype(BF16)

    @pl.when(jnp.logical_not(used))
    def _():
        ys_ref[...] = jnp.zeros_like(ys_ref)

    @pl.when(used)
    def _():
        x = _load_row_tiled(xs_ref).astype(BF16)
        g = jnp.dot(x, wg_bf[...], preferred_element_type=F32) + bg_ref[0]
        u = jnp.dot(x, wu_bf[...], preferred_element_type=F32) + bu_ref[0]
        g = jnp.minimum(g, SWIGLU_LIMIT)
        u = jnp.clip(u, -SWIGLU_LIMIT, SWIGLU_LIMIT)
        mid = (u + 1.0) * (g * (1.0 / (1.0 + jnp.exp(-SWIGLU_ALPHA * g))))
        _store_row_tiled(ys_ref, jnp.dot(mid.astype(BF16), wd_bf[...],
                                         preferred_element_type=F32) + bd_ref[0])


def _moe(block_expert, n_used, xs, wg, bg, wu, bu, wd, bd):
    n_slots = xs.shape[0] // ROW_TILE
    blk = lambda i, be, nu: (jnp.minimum(i, nu[0] - 1), 0)
    wsel = lambda i, be, nu: (be[jnp.minimum(i, nu[0] - 1)], 0, 0)
    wspec = pl.BlockSpec((1, D_MODEL, D_EXPERT), wsel)
    bspec = pl.BlockSpec((1, 1, D_EXPERT), wsel)
    return pl.pallas_call(
        _moe_kernel,
        grid_spec=pltpu.PrefetchScalarGridSpec(
            num_scalar_prefetch=2,
            grid=(n_slots // TM_MOE,),
            in_specs=[pl.BlockSpec((TM_MOE * ROW_TILE, LANES), blk),
                      wspec, bspec, wspec, bspec, wspec, bspec],
            out_specs=pl.BlockSpec((TM_MOE * ROW_TILE, LANES), lambda i, be, nu: (i, 0)),
            scratch_shapes=[pltpu.VMEM((D_MODEL, D_EXPERT), BF16)] * 3),
        out_shape=jax.ShapeDtypeStruct((n_slots * ROW_TILE, LANES), F32),
        compiler_params=_cparams(("arbitrary",)),
        name="moe_experts",
    )(block_expert, n_used, xs, wg, bg, wu, bu, wd, bd)


def _combine_kernel(dest_ref, x1_ref, gate_ref, ys_ref, out_ref, idx0_ref, idx1_ref, buf_ref,
                    idx_sem, row_sem):
    i = pl.program_id(0)
    n_tiles = pl.num_programs(0)
    slot = i % 2
    idx_refs = (idx0_ref, idx1_ref)

    def idx_copy(tile, sl):
        return pltpu.make_async_copy(dest_ref.at[tile], idx_refs[sl], idx_sem.at[sl])

    def gather_rows(sl):
        def body(r, carry):
            for k in range(TOP_K):
                d = idx_refs[sl][k * T_ROWS + r]
                pltpu.make_async_copy(ys_ref.at[_tile_rows(d, 1)],
                                      buf_ref.at[sl, k, _tile_rows(r, 1)],
                                      row_sem.at[sl]).start()
            return carry

        lax.fori_loop(0, T_ROWS, body, 0)

    @pl.when(i == 0)
    def _():
        idx_copy(0, 0).start()
        idx_copy(0, 0).wait()
        gather_rows(0)

        @pl.when(n_tiles > 1)
        def _():
            idx_copy(1, 1).start()

    for sl in range(2):
        @pl.when(slot == sl)
        def _(sl=sl):
            @pl.when(i + 1 < n_tiles)
            def _():
                idx_copy(i + 1, 1 - sl).wait()
                gather_rows(1 - sl)

            @pl.when(i + 2 < n_tiles)
            def _():
                idx_copy(i + 2, sl).start()

            _row_copies_wait(buf_ref.at[sl, 0], ys_ref, row_sem.at[sl])
            gate = gate_ref[...]
            acc = x1_ref[...]
            for k in range(TOP_K):
                acc = acc + gate[:, k:k + 1] * _load_row_tiled(buf_ref.at[sl, k])
            out_ref[...] = acc


def _combine(dest_tiles, x1, gates_t, ys):
    n = x1.shape[0]
    row = lambda i: (i, 0)
    return pl.pallas_call(
        _combine_kernel,
        grid=(n // T_ROWS,),
        in_specs=[pl.BlockSpec(memory_space=pl.ANY),
                  pl.BlockSpec((T_ROWS, D_MODEL), row),
                  pl.BlockSpec((T_ROWS, TOP_K), row),
                  pl.BlockSpec(memory_space=pl.ANY)],
        out_specs=pl.BlockSpec((T_ROWS, D_MODEL), row),
        out_shape=jax.ShapeDtypeStruct((n, D_MODEL), F32),
        scratch_shapes=[pltpu.SMEM((TOP_K * T_ROWS,), jnp.int32),
                        pltpu.SMEM((TOP_K * T_ROWS,), jnp.int32),
                        pltpu.VMEM((2, TOP_K, T_ROWS * ROW_TILE, LANES), F32),
                        pltpu.SemaphoreType.DMA((2,)), pltpu.SemaphoreType.DMA((2,))],
        compiler_params=_cparams(("arbitrary",)),
        name="combine",
    )(dest_tiles, x1, gates_t, ys)


def _rope_tables(seq):
    def freqs(dim):
        return ROPE_THETA ** (-(jnp.arange(0, dim, 2, dtype=F32) / dim))

    def table(ang):
        cos = jnp.cos(ang)
        sin = jnp.sin(ang)
        cos64 = jnp.concatenate([cos, cos], axis=-1)
        sin64 = jnp.concatenate([-sin, sin], axis=-1)
        return jnp.tile(cos64, (1, 2)), jnp.tile(sin64, (1, 2))

    t = jnp.arange(seq, dtype=F32)
    ang_1d = t[:, None] * freqs(HEAD_DIM)[None, :]
    rows = seq // GRID_W
    rowp = jnp.repeat(jnp.arange(rows, dtype=F32), GRID_W)
    colp = jnp.tile(jnp.arange(GRID_W, dtype=F32), rows)
    f_ax = freqs(HEAD_DIM // 2)
    ang_2d = jnp.concatenate([rowp[:, None] * f_ax[None, :], colp[:, None] * f_ax[None, :]],
                             axis=-1)
    return table(ang_1d) + table(ang_2d)


def _tile2(g):
    return jnp.tile(g.astype(F32), 2)[None, :]


def kernel(x, norm1_g, w_in, a_qnorm_g, a_knorm_g, a_lq1, a_lk1, a_lq2, a_lk2, a_subln_g,
           b_qnorm_g, b_knorm_g, w_out, norm2_g, router_w, router_b,
           w_gate, b_gate, w_up, b_up, w_down, b_down):
    b, s, d = x.shape
    n = b * s
    layer = 0
    x2 = x.reshape(n, d)
    cosa, sina, cosb, sinb = _rope_tables(s)

    qa, kat, va, qb, kbt, vb = _inproj(
        x2, norm1_g[layer][None, :], w_in[layer],
        _tile2(a_qnorm_g[layer]), _tile2(a_knorm_g[layer]),
        _tile2(b_qnorm_g[layer]), _tile2(b_knorm_g[layer]),
        cosa, sina, cosb, sinb, b, s)

    oa = _attn_a(a_lq1[layer][None, :], a_lk1[layer][None, :], a_lq2[layer][None, :],
                 a_lk2[layer][None, :], a_subln_g[layer][None, :],
                 qa.reshape(b, s, A_W), kat, va.reshape(b, s, A_W))
    ob = _attn_b(qb.reshape(b, s, A_W), kbt, vb.reshape(b, s, B_KV_W))

    rwt = router_w[layer].T.astype(F32)
    rwh = rwt.astype(BF16)
    rwl = (rwt - rwh.astype(F32)).astype(BF16)
    x1, h2, topi, gates, rank, counts = _outproj(
        oa.reshape(n, A_W), ob.reshape(n, A_W), x2, w_out[layer],
        norm2_g[layer][None, :], rwh, rwl, router_b[layer].astype(F32)[:, None])

    counts = counts[:, 0]
    padded = ((counts + TM_MOE - 1) // TM_MOE) * TM_MOE
    pad_end = jnp.cumsum(padded)
    pad_start = pad_end - padded
    eids = jnp.arange(N_EXPERTS, dtype=jnp.int32)
    start_of = jnp.sum(jnp.where(topi[:, :, None] == eids, pad_start, 0), axis=-1)
    dest = start_of + rank
    n_tiles = n // T_ROWS
    dest_tiles = dest.reshape(TOP_K, n_tiles, T_ROWS).transpose(1, 0, 2).reshape(
        n_tiles, TOP_K * T_ROWS)
    n_slots = n * TOP_K + N_EXPERTS * TM_MOE
    n_blocks = n_slots // TM_MOE
    block_start = jnp.arange(n_blocks, dtype=jnp.int32) * TM_MOE
    block_expert = jnp.minimum(
        jnp.sum((block_start[:, None] >= pad_end[None, :]).astype(jnp.int32), axis=1),
        N_EXPERTS - 1)
    n_used = (pad_end[-1:] // TM_MOE).astype(jnp.int32)

    zero_lo = jnp.concatenate([jnp.maximum(pad_end - TM_MOE, pad_start), pad_end[-1:]])
    zero_hi = jnp.concatenate([pad_end, jnp.full((1,), n_slots, jnp.int32)])
    xs = _dispatch(zero_lo.astype(jnp.int32), zero_hi.astype(jnp.int32), dest_tiles, h2,
                   n_slots)
    ys = _moe(block_expert, n_used, xs,
              w_gate[layer], b_gate[layer][:, None, :],
              w_up[layer], b_up[layer][:, None, :],
              w_down[layer], b_down[layer][:, None, :])
    out = _combine(dest_tiles, x1, gates.T, ys)
    return out.reshape(b, s, d)
```

```python
import math

import jax
import jax.numpy as jnp
from jax import lax
from jax.experimental import pallas as pl
from jax.experimental.pallas import tpu as pltpu

F32 = jnp.float32
BF16 = jnp.bfloat16

D_MODEL = 1024
HEAD_DIM = 64
A_HEADS = 4
B_HEADS = 8
B_KV_HEADS = 2
B_GROUP = B_HEADS // B_KV_HEADS
A_W = 512
B_KV_W = 128
D_IN_PROJ = 4 * A_W + 2 * B_KV_W
GRID_W = 64
ROPE_THETA = 10000.0
EPS = 1e-6
N_EXPERTS = 32
TOP_K = 4
D_EXPERT = 1024
SWIGLU_LIMIT = 7.0
SWIGLU_ALPHA = 1.702
LAM_INIT = 0.8 - 0.6 * math.exp(-0.3 * 0)

LANES = 128
ROW_TILE = D_MODEL // LANES
VMEM_LIMIT = 56 * 1024 * 1024

TM_PROJ = 512
TQ = 256
KC = 512
A_HEADS_PER_STEP = 4
TM_MOE = 512
T_ROWS = 512
ZERO_CHUNK = 64

def _cparams(sem):
    return pltpu.CompilerParams(dimension_semantics=sem, vmem_limit_bytes=VMEM_LIMIT)


def _head_norm_rope(p, gain, cos, sin, scale):
    tm, w = p.shape
    cw = min(w, 256)
    rep = cw // LANES
    r = lax.broadcasted_iota(jnp.int32, (cw, cw), 0) // HEAD_DIM
    c = lax.broadcasted_iota(jnp.int32, (cw, cw), 1) // HEAD_DIM
    ones_blk = jnp.where(r == c, 1.0, 0.0).astype(BF16)
    lane = lax.broadcasted_iota(jnp.int32, (tm, cw), 1)
    first_half = (lane % HEAD_DIM) < (HEAD_DIM // 2)
    gain_t = jnp.tile(gain, (1, rep))
    cos_t = jnp.tile(cos, (1, rep))
    sin_t = jnp.tile(sin, (1, rep))
    outs = []
    for c0 in range(0, w, cw):
        pc = p[:, c0:c0 + cw]
        ssum = jnp.dot((pc * pc).astype(BF16), ones_blk, preferred_element_type=F32)
        xn = pc * lax.rsqrt(ssum * (1.0 / HEAD_DIM) + EPS) * gain_t
        partner = jnp.where(first_half, pltpu.roll(xn, cw - HEAD_DIM // 2, 1),
                            pltpu.roll(xn, HEAD_DIM // 2, 1))
        outs.append((xn * cos_t + partner * sin_t) * scale)
    return outs, cw


def _inproj_kernel(x_ref, g1_ref, w_ref, gqa_ref, gka_ref, gqb_ref, gkb_ref,
                   cosa_ref, sina_ref, cosb_ref, sinb_ref,
                   qa_ref, kat_ref, va_ref, qb_ref, kbt_ref, vb_ref, wbf_ref):
    @pl.when(pl.program_id(0) == 0)
    def _():
        wbf_ref[...] = w_ref[...].astype(BF16)

    x = x_ref[...]
    ms = jnp.mean(x * x, axis=-1, keepdims=True)
    h = (x * lax.rsqrt(ms + EPS) * g1_ref[...]).astype(BF16)
    proj = jnp.dot(h, wbf_ref[...], preferred_element_type=F32)
    q_scale = HEAD_DIM ** -0.5 * math.log2(math.e)
    o = 0
    for ref, gain, cos, sin, scale, w, transposed in (
            (qa_ref, gqa_ref, cosa_ref, sina_ref, q_scale, A_W, False),
            (kat_ref, gka_ref, cosa_ref, sina_ref, 1.0, A_W, True),
            (va_ref, None, None, None, None, A_W, False),
            (qb_ref, gqb_ref, cosb_ref, sinb_ref, q_scale, A_W, False),
            (kbt_ref, gkb_ref, cosb_ref, sinb_ref, 1.0, B_KV_W, True),
            (vb_ref, None, None, None, None, B_KV_W, False)):
        p = proj[:, o:o + w]
        if gain is None:
            ref[...] = p.astype(BF16)
        else:
            outs, cw = _head_norm_rope(p, gain[...], cos[...], sin[...], scale)
            for j, y in enumerate(outs):
                if transposed:
                    ref[0, j * cw:(j + 1) * cw, :] = y.T.astype(BF16)
                else:
                    ref[:, j * cw:(j + 1) * cw] = y.astype(BF16)
        o += w


def _inproj(x2, g1, w_in, gqa, gka, gqb, gkb, cosa, sina, cosb, sinb, batch, seq):
    n = x2.shape[0]
    tm = TM_PROJ
    per_seq = seq // tm
    row = lambda i: (i, 0)
    const = lambda i: (0, 0)
    tab = lambda i: (i % per_seq, 0)
    ktr = lambda i: (i // per_seq, 0, i % per_seq)
    return pl.pallas_call(
        _inproj_kernel,
        grid=(n // tm,),
        in_specs=[pl.BlockSpec((tm, D_MODEL), row),
                  pl.BlockSpec((1, D_MODEL), const),
                  pl.BlockSpec((D_MODEL, D_IN_PROJ), const)]
                 + [pl.BlockSpec((1, LANES), const)] * 4
                 + [pl.BlockSpec((tm, LANES), tab)] * 4,
        out_specs=[pl.BlockSpec((tm, A_W), row),
                   pl.BlockSpec((1, A_W, tm), ktr),
                   pl.BlockSpec((tm, A_W), row),
                   pl.BlockSpec((tm, A_W), row),
                   pl.BlockSpec((1, B_KV_W, tm), ktr),
                   pl.BlockSpec((tm, B_KV_W), row)],
        out_shape=[jax.ShapeDtypeStruct((n, A_W), BF16),
                   jax.ShapeDtypeStruct((batch, A_W, seq), BF16),
                   jax.ShapeDtypeStruct((n, A_W), BF16),
                   jax.ShapeDtypeStruct((n, A_W), BF16),
                   jax.ShapeDtypeStruct((batch, B_KV_W, seq), BF16),
                   jax.ShapeDtypeStruct((n, B_KV_W), BF16)],
        scratch_shapes=[pltpu.VMEM((D_MODEL, D_IN_PROJ), BF16)],
        compiler_params=_cparams(("arbitrary",)),
        name="inproj",
    )(x2, g1, w_in, gqa, gka, gqb, gkb, cosa, sina, cosb, sinb)


def _softmax_pv_pipeline(queries, key_chunk, value_chunk, s_ref, seq):
    tq = queries[0].shape[0]
    n_chunks = seq // KC
    lanes_per_chunk = KC // LANES

    def qk_chunk(n, c, run_max):
        s = jnp.dot(queries[n], key_chunk(n, c), preferred_element_type=F32)
        s_ref[n % 2, :, c * KC:(c + 1) * KC] = s
        for k in range(lanes_per_chunk):
            run_max = jnp.maximum(run_max, s[:, k * LANES:(k + 1) * LANES])
        return run_max

    def pv_chunk(n, c, m, run_sum, acc):
        parts = []
        for k in range(lanes_per_chunk):
            lo = c * KC + k * LANES
            e = jnp.exp2(s_ref[n % 2, :, lo:lo + LANES] - m)
            run_sum = run_sum + e
            parts.append(e.astype(BF16))
        d = jnp.dot(jnp.concatenate(parts, axis=-1), value_chunk(n, c),
                    preferred_element_type=F32)
        return run_sum, d if acc is None else acc + d

    neg = jnp.full((tq, LANES), -jnp.inf, F32)
    run_max = neg
    for c in range(n_chunks):
        run_max = qk_chunk(0, c, run_max)
    results = []
    for n in range(len(queries)):
        m = jnp.broadcast_to(jnp.max(run_max, axis=-1, keepdims=True), (tq, LANES))
        run_sum = jnp.zeros((tq, LANES), F32)
        acc = None
        run_max = neg
        for c in range(n_chunks):
            if n + 1 < len(queries):
                run_max = qk_chunk(n + 1, c, run_max)
            run_sum, acc = pv_chunk(n, c, m, run_sum, acc)
        results.append((acc, jnp.sum(run_sum, axis=-1, keepdims=True)))
    return results


def _attn_a_kernel(lq1_ref, lk1_ref, lq2_ref, lk2_ref, gsub_ref,
                   q_ref, kt_ref, v_ref, o_ref, s_ref):
    seq = v_ref.shape[1]
    q = q_ref[0]
    n_mats = 2 * A_HEADS_PER_STEP
    queries = [q[:, n * HEAD_DIM:(n + 1) * HEAD_DIM] for n in range(n_mats)]
    results = _softmax_pv_pipeline(
        queries,
        lambda n, c: kt_ref[0, n * HEAD_DIM:(n + 1) * HEAD_DIM, c * KC:(c + 1) * KC],
        lambda n, c: v_ref[0, c * KC:(c + 1) * KC, (n // 2) * LANES:(n // 2 + 1) * LANES],
        s_ref, seq)
    lam = (jnp.exp(jnp.sum(lq1_ref[...] * lk1_ref[...], axis=-1, keepdims=True))
           - jnp.exp(jnp.sum(lq2_ref[...] * lk2_ref[...], axis=-1, keepdims=True))
           + LAM_INIT)
    for h in range(A_HEADS_PER_STEP):
        (o1, l1), (o2, l2) = results[2 * h], results[2 * h + 1]
        o = o1 / l1 - lam * (o2 / l2)
        ms = jnp.mean(o * o, axis=-1, keepdims=True)
        o = o * lax.rsqrt(ms + EPS) * gsub_ref[...] * (1.0 - LAM_INIT)
        o_ref[0, :, h * LANES:(h + 1) * LANES] = o.astype(BF16)


def _attn_a(lq1, lk1, lq2, lk2, gsub, qa, kat, va):
    b, s, _ = qa.shape
    w = A_HEADS_PER_STEP * 2 * HEAD_DIM
    vec = lambda width: pl.BlockSpec((1, width), lambda bi, h, i: (0, 0))
    return pl.pallas_call(
        _attn_a_kernel,
        grid=(b, A_HEADS // A_HEADS_PER_STEP, s // TQ),
        in_specs=[vec(HEAD_DIM)] * 4 + [vec(LANES)]
                 + [pl.BlockSpec((1, TQ, w), lambda bi, h, i: (bi, i, h)),
                    pl.BlockSpec((1, w, s), lambda bi, h, i: (bi, h, 0)),
                    pl.BlockSpec((1, s, w), lambda bi, h, i: (bi, 0, h))],
        out_specs=pl.BlockSpec((1, TQ, w), lambda bi, h, i: (bi, i, h)),
        out_shape=jax.ShapeDtypeStruct((b, s, A_W), BF16),
        scratch_shapes=[pltpu.VMEM((2, TQ, s), F32)],
        compiler_params=_cparams(("arbitrary",) * 3),
        name="attn_a",
    )(lq1, lk1, lq2, lk2, gsub, qa, kat, va)


def _attn_b_kernel(q_ref, kt_ref, v_ref, o_ref, s_ref):
    seq = v_ref.shape[1]
    q = q_ref[0]
    queries = [q[:, j * HEAD_DIM:(j + 1) * HEAD_DIM] for j in range(B_HEADS)]
    results = _softmax_pv_pipeline(
        queries,
        lambda n, c: kt_ref[0, (n // B_GROUP) * HEAD_DIM:(n // B_GROUP + 1) * HEAD_DIM,
                            c * KC:(c + 1) * KC],
        lambda n, c: v_ref[0, c * KC:(c + 1) * KC, :],
        s_ref, seq)
    low = lax.broadcasted_iota(jnp.int32, (q.shape[0], LANES), 1) < HEAD_DIM
    for pair in range(B_HEADS // 2):
        halves = []
        for side in range(2):
            j = 2 * pair + side
            acc, l = results[j]
            r = acc / l
            halves.append(r if j // B_GROUP == side else pltpu.roll(r, HEAD_DIM, 1))
        o_ref[0, :, pair * LANES:(pair + 1) * LANES] = (
            jnp.where(low, halves[0], halves[1]).astype(BF16))


def _attn_b(qb, kbt, vb):
    b, s, _ = qb.shape
    return pl.pallas_call(
        _attn_b_kernel,
        grid=(b, s // TQ),
        in_specs=[pl.BlockSpec((1, TQ, A_W), lambda bi, i: (bi, i, 0)),
                  pl.BlockSpec((1, B_KV_W, s), lambda bi, i: (bi, 0, 0)),
                  pl.BlockSpec((1, s, B_KV_W), lambda bi, i: (bi, 0, 0))],
        out_specs=pl.BlockSpec((1, TQ, A_W), lambda bi, i: (bi, i, 0)),
        out_shape=jax.ShapeDtypeStruct((b, s, A_W), BF16),
        scratch_shapes=[pltpu.VMEM((2, TQ, s), F32)],
        compiler_params=_cparams(("arbitrary",) * 2),
        name="attn_b",
    )(qb, kbt, vb)


def _store_row_tiled(ref, val):
    t = val.shape[0]
    for s in range(ROW_TILE):
        ref[pl.ds(s, t, stride=ROW_TILE), :] = val[:, s * LANES:(s + 1) * LANES]


def _load_row_tiled(ref):
    t = ref.shape[0] // ROW_TILE
    return jnp.concatenate([ref[pl.ds(s, t, stride=ROW_TILE), :] for s in range(ROW_TILE)],
                           axis=1)


def _tile_rows(row, count):
    return pl.ds(pl.multiple_of(row * ROW_TILE, ROW_TILE), count * ROW_TILE)


def _outproj_kernel(oa_ref, ob_ref, x_ref, wo_ref, g2_ref, rwh_ref, rwl_ref, rb_ref,
                    x1_ref, h2_ref, topi_ref, gate_ref, rank_ref, cnt_ref,
                    carry_ref, wbf_ref):
    i = pl.program_id(0)

    @pl.when(i == 0)
    def _():
        carry_ref[...] = jnp.zeros_like(carry_ref)
        wbf_ref[...] = wo_ref[...].astype(BF16)

    tm = x_ref.shape[0]
    a = (jnp.dot(oa_ref[...], wbf_ref[:A_W, :], preferred_element_type=F32)
         + jnp.dot(ob_ref[...], wbf_ref[A_W:, :], preferred_element_type=F32))
    x1 = x_ref[...] + a
    x1_ref[...] = x1
    ms = jnp.mean(x1 * x1, axis=-1, keepdims=True)
    h2 = x1 * lax.rsqrt(ms + EPS) * g2_ref[...]
    _store_row_tiled(h2_ref, h2)

    hi = h2.astype(BF16)
    lo = (h2 - hi.astype(F32)).astype(BF16)
    nt = (((1,), (1,)), ((), ()))
    rwh = rwh_ref[...]
    logits = (lax.dot_general(rwh, hi, nt, preferred_element_type=F32)
              + lax.dot_general(rwh, lo, nt, preferred_element_type=F32)
              + lax.dot_general(rwl_ref[...], hi, nt, preferred_element_type=F32)
              + rb_ref[...])

    eio = lax.broadcasted_iota(jnp.int32, logits.shape, 0)
    work = logits
    vals, idxs = [], []
    for _ in range(TOP_K):
        mx = jnp.max(work, axis=0, keepdims=True)
        idx = jnp.min(jnp.where(work == mx, eio, N_EXPERTS), axis=0, keepdims=True)
        vals.append(mx)
        idxs.append(idx)
        work = jnp.where(eio == idx, -jnp.inf, work)
    exps = [jnp.exp(v - vals[0]) for v in vals]
    tot = exps[0] + exps[1] + exps[2] + exps[3]

    member = jnp.zeros(logits.shape, F32)
    for idx in idxs:
        member = member + jnp.where(eio == idx, 1.0, 0.0)
    r = lax.broadcasted_iota(jnp.int32, (tm, tm), 0)
    c = lax.broadcasted_iota(jnp.int32, (tm, tm), 1)
    upper = jnp.where(r < c, 1.0, 0.0).astype(BF16)
    pos = jnp.dot(member.astype(BF16), upper, preferred_element_type=F32) + carry_ref[...]
    for k in range(TOP_K):
        topi_ref[k:k + 1, :] = idxs[k]
        gate_ref[k:k + 1, :] = exps[k] / tot
        rank_ref[k:k + 1, :] = jnp.sum(jnp.where(eio == idxs[k], pos, 0.0), axis=0,
                                       keepdims=True).astype(jnp.int32)
    carry = carry_ref[...] + jnp.sum(member, axis=1, keepdims=True)
    carry_ref[...] = carry
    cnt_ref[...] = carry.astype(jnp.int32)


def _outproj(oa, ob, x2, wo, g2, rwh, rwl, rb):
    n = x2.shape[0]
    tm = TM_PROJ
    row = lambda i: (i, 0)
    col = lambda i: (0, i)
    const = lambda i: (0, 0)
    return pl.pallas_call(
        _outproj_kernel,
        grid=(n // tm,),
        in_specs=[pl.BlockSpec((tm, A_W), row), pl.BlockSpec((tm, A_W), row),
                  pl.BlockSpec((tm, D_MODEL), row),
                  pl.BlockSpec((2 * A_W, D_MODEL), const),
                  pl.BlockSpec((1, D_MODEL), const),
                  pl.BlockSpec((N_EXPERTS, D_MODEL), const),
                  pl.BlockSpec((N_EXPERTS, D_MODEL), const),
                  pl.BlockSpec((N_EXPERTS, 1), const)],
        out_specs=[pl.BlockSpec((tm, D_MODEL), row), pl.BlockSpec((tm * ROW_TILE, LANES), row),
                   pl.BlockSpec((TOP_K, tm), col), pl.BlockSpec((TOP_K, tm), col),
                   pl.BlockSpec((TOP_K, tm), col),
                   pl.BlockSpec((N_EXPERTS, 1), const)],
        out_shape=[jax.ShapeDtypeStruct((n, D_MODEL), F32),
                   jax.ShapeDtypeStruct((n * ROW_TILE, LANES), F32),
                   jax.ShapeDtypeStruct((TOP_K, n), jnp.int32),
                   jax.ShapeDtypeStruct((TOP_K, n), F32),
                   jax.ShapeDtypeStruct((TOP_K, n), jnp.int32),
                   jax.ShapeDtypeStruct((N_EXPERTS, 1), jnp.int32)],
        scratch_shapes=[pltpu.VMEM((N_EXPERTS, 1), F32),
                        pltpu.VMEM((2 * A_W, D_MODEL), BF16)],
        compiler_params=_cparams(("arbitrary",)),
        name="outproj_router",
    )(oa, ob, x2, wo, g2, rwh, rwl, rb)


def _row_copies_wait(like_ref, hbm_ref, sem):
    for _ in range(TOP_K):
        pltpu.make_async_copy(like_ref, hbm_ref.at[_tile_rows(0, T_ROWS)], sem).wait()


def _dispatch_kernel(zlo_ref, zhi_ref, dest_ref, h2_ref, xs_ref,
                     idx0_ref, idx1_ref, zero_ref, idx_sem, row_sem, zero_sem):
    i = pl.program_id(0)
    slot = i % 2
    idx_refs = (idx0_ref, idx1_ref)

    def idx_copy(tile, sl):
        return pltpu.make_async_copy(dest_ref.at[tile], idx_refs[sl], idx_sem.at[sl])

    @pl.when(i == 0)
    def _():
        idx_copy(0, 0).start()
        zero_ref[...] = jnp.zeros_like(zero_ref)

        def chunk_copy(c):
            return pltpu.make_async_copy(
                zero_ref, xs_ref.at[_tile_rows(c * ZERO_CHUNK, ZERO_CHUNK)], zero_sem)

        def issue_chunk(c, carry):
            chunk_copy(c).start()
            return carry

        def drain_chunk(c, carry):
            chunk_copy(c).wait()
            return carry

        def per_range(e, carry):
            lo = zlo_ref[e] // ZERO_CHUNK
            hi = zhi_ref[e] // ZERO_CHUNK
            lax.fori_loop(lo, hi, issue_chunk, 0)
            lax.fori_loop(lo, hi, drain_chunk, 0)
            return carry

        lax.fori_loop(0, N_EXPERTS + 1, per_range, 0)

    for sl in range(2):
        @pl.when(slot == sl)
        def _(sl=sl):
            @pl.when(i + 1 < pl.num_programs(0))
            def _():
                idx_copy(i + 1, 1 - sl).start()

            idx_copy(i, sl).wait()

            def body(r, carry):
                for k in range(TOP_K):
                    d = idx_refs[sl][k * T_ROWS + r]
                    pltpu.make_async_copy(h2_ref.at[_tile_rows(r, 1)], xs_ref.at[_tile_rows(d, 1)],
                                          row_sem).start(priority=k % 2)
                return carry

            lax.fori_loop(0, T_ROWS, body, 0)

    _row_copies_wait(h2_ref, xs_ref, row_sem)


def _dispatch(zero_lo, zero_hi, dest_tiles, h2, n_slots):
    n = h2.shape[0] // ROW_TILE
    return pl.pallas_call(
        _dispatch_kernel,
        grid_spec=pltpu.PrefetchScalarGridSpec(
            num_scalar_prefetch=2,
            grid=(n // T_ROWS,),
            in_specs=[pl.BlockSpec(memory_space=pl.ANY),
                      pl.BlockSpec((T_ROWS * ROW_TILE, LANES), lambda i, zl, zh: (i, 0))],
            out_specs=pl.BlockSpec(memory_space=pl.ANY),
            scratch_shapes=[pltpu.SMEM((TOP_K * T_ROWS,), jnp.int32),
                            pltpu.SMEM((TOP_K * T_ROWS,), jnp.int32),
                            pltpu.VMEM((ZERO_CHUNK * ROW_TILE, LANES), F32),
                            pltpu.SemaphoreType.DMA((2,)), pltpu.SemaphoreType.DMA,
                            pltpu.SemaphoreType.DMA]),
        out_shape=jax.ShapeDtypeStruct((n_slots * ROW_TILE, LANES), F32),
        compiler_params=_cparams(("arbitrary",)),
        name="dispatch",
    )(zero_lo, zero_hi, dest_tiles, h2)


def _moe_kernel(be_ref, nused_ref, xs_ref, wg_ref, bg_ref, wu_ref, bu_ref, wd_ref, bd_ref,
                ys_ref, wg_bf, wu_bf, wd_bf):
    i = pl.program_id(0)
    used = i < nused_ref[0]
    prev = be_ref[jnp.maximum(i - 1, 0)]

    @pl.when(used & ((i == 0) | (be_ref[i] != prev)))
    def _():
        wg_bf[...] = wg_ref[0].astype(BF16)
        wu_bf[...] = wu_ref[0].astype(BF16)
        wd_bf[...] = wd_ref[0].astype(BF16)

    @pl.when(jnp.logical_not(used))
    def _():
        ys_ref[...] = jnp.zeros_like(ys_ref)

    @pl.when(used)
    def _():
        x = _load_row_tiled(xs_ref).astype(BF16)
        g = jnp.dot(x, wg_bf[...], preferred_element_type=F32) + bg_ref[0]
        u = jnp.dot(x, wu_bf[...], preferred_element_type=F32) + bu_ref[0]
        g = jnp.minimum(g, SWIGLU_LIMIT)
        u = jnp.clip(u, -SWIGLU_LIMIT, SWIGLU_LIMIT)
        mid = (u + 1.0) * (g * (1.0 / (1.0 + jnp.exp(-SWIGLU_ALPHA * g))))
        _store_row_tiled(ys_ref, jnp.dot(mid.astype(BF16), wd_bf[...],
                                         preferred_element_type=F32) + bd_ref[0])


def _moe(block_expert, n_used, xs, wg, bg, wu, bu, wd, bd):
    n_slots = xs.shape[0] // ROW_TILE
    blk = lambda i, be, nu: (jnp.minimum(i, nu[0] - 1), 0)
    wsel = lambda i, be, nu: (be[jnp.minimum(i, nu[0] - 1)], 0, 0)
    wspec = pl.BlockSpec((1, D_MODEL, D_EXPERT), wsel)
    bspec = pl.BlockSpec((1, 1, D_EXPERT), wsel)
    return pl.pallas_call(
        _moe_kernel,
        grid_spec=pltpu.PrefetchScalarGridSpec(
            num_scalar_prefetch=2,
            grid=(n_slots // TM_MOE,),
            in_specs=[pl.BlockSpec((TM_MOE * ROW_TILE, LANES), blk),
                      wspec, bspec, wspec, bspec, wspec, bspec],
            out_specs=pl.BlockSpec((TM_MOE * ROW_TILE, LANES), lambda i, be, nu: (i, 0)),
            scratch_shapes=[pltpu.VMEM((D_MODEL, D_EXPERT), BF16)] * 3),
        out_shape=jax.ShapeDtypeStruct((n_slots * ROW_TILE, LANES), F32),
        compiler_params=_cparams(("arbitrary",)),
        name="moe_experts",
    )(block_expert, n_used, xs, wg, bg, wu, bu, wd, bd)


def _combine_kernel(dest_ref, x1_ref, gate_ref, ys_ref, out_ref, idx0_ref, idx1_ref, buf_ref,
                    idx_sem, row_sem):
    i = pl.program_id(0)
    n_tiles = pl.num_programs(0)
    slot = i % 2
    idx_refs = (idx0_ref, idx1_ref)

    def idx_copy(tile, sl):
        return pltpu.make_async_copy(dest_ref.at[tile], idx_refs[sl], idx_sem.at[sl])

    def gather_rows(sl):
        def body(r, carry):
            for k in range(TOP_K):
                d = idx_refs[sl][k * T_ROWS + r]
                pltpu.make_async_copy(ys_ref.at[_tile_rows(d, 1)],
                                      buf_ref.at[sl, k, _tile_rows(r, 1)],
                                      row_sem.at[sl]).start(priority=k % 2)
            return carry

        lax.fori_loop(0, T_ROWS, body, 0)

    @pl.when(i == 0)
    def _():
        idx_copy(0, 0).start()
        idx_copy(0, 0).wait()
        gather_rows(0)

        @pl.when(n_tiles > 1)
        def _():
            idx_copy(1, 1).start()

    for sl in range(2):
        @pl.when(slot == sl)
        def _(sl=sl):
            @pl.when(i + 1 < n_tiles)
            def _():
                idx_copy(i + 1, 1 - sl).wait()
                gather_rows(1 - sl)

            @pl.when(i + 2 < n_tiles)
            def _():
                idx_copy(i + 2, sl).start()

            _row_copies_wait(buf_ref.at[sl, 0], ys_ref, row_sem.at[sl])
            gate = gate_ref[...]
            acc = x1_ref[...]
            for k in range(TOP_K):
                acc = acc + gate[:, k:k + 1] * _load_row_tiled(buf_ref.at[sl, k])
            out_ref[...] = acc


def _combine(dest_tiles, x1, gates_t, ys):
    n = x1.shape[0]
    row = lambda i: (i, 0)
    return pl.pallas_call(
        _combine_kernel,
        grid=(n // T_ROWS,),
        in_specs=[pl.BlockSpec(memory_space=pl.ANY),
                  pl.BlockSpec((T_ROWS, D_MODEL), row),
                  pl.BlockSpec((T_ROWS, TOP_K), row),
                  pl.BlockSpec(memory_space=pl.ANY)],
        out_specs=pl.BlockSpec((T_ROWS, D_MODEL), row),
        out_shape=jax.ShapeDtypeStruct((n, D_MODEL), F32),
        scratch_shapes=[pltpu.SMEM((TOP_K * T_ROWS,), jnp.int32),
                        pltpu.SMEM((TOP_K * T_ROWS,), jnp.int32),
                        pltpu.VMEM((2, TOP_K, T_ROWS * ROW_TILE, LANES), F32),
                        pltpu.SemaphoreType.DMA((2,)), pltpu.SemaphoreType.DMA((2,))],
        compiler_params=_cparams(("arbitrary",)),
        name="combine",
    )(dest_tiles, x1, gates_t, ys)


def _rope_tables(seq):
    def freqs(dim):
        return ROPE_THETA ** (-(jnp.arange(0, dim, 2, dtype=F32) / dim))

    def table(ang):
        cos = jnp.cos(ang)
        sin = jnp.sin(ang)
        cos64 = jnp.concatenate([cos, cos], axis=-1)
        sin64 = jnp.concatenate([-sin, sin], axis=-1)
        return jnp.tile(cos64, (1, 2)), jnp.tile(sin64, (1, 2))

    t = jnp.arange(seq, dtype=F32)
    ang_1d = t[:, None] * freqs(HEAD_DIM)[None, :]
    rows = seq // GRID_W
    rowp = jnp.repeat(jnp.arange(rows, dtype=F32), GRID_W)
    colp = jnp.tile(jnp.arange(GRID_W, dtype=F32), rows)
    f_ax = freqs(HEAD_DIM // 2)
    ang_2d = jnp.concatenate([rowp[:, None] * f_ax[None, :], colp[:, None] * f_ax[None, :]],
                             axis=-1)
    return table(ang_1d) + table(ang_2d)


def _tile2(g):
    return jnp.tile(g.astype(F32), 2)[None, :]


def kernel(x, norm1_g, w_in, a_qnorm_g, a_knorm_g, a_lq1, a_lk1, a_lq2, a_lk2, a_subln_g,
           b_qnorm_g, b_knorm_g, w_out, norm2_g, router_w, router_b,
           w_gate, b_gate, w_up, b_up, w_down, b_down):
    b, s, d = x.shape
    n = b * s
    layer = 0
    x2 = x.reshape(n, d)
    cosa, sina, cosb, sinb = _rope_tables(s)

    qa, kat, va, qb, kbt, vb = _inproj(
        x2, norm1_g[layer][None, :], w_in[layer],
        _tile2(a_qnorm_g[layer]), _tile2(a_knorm_g[layer]),
        _tile2(b_qnorm_g[layer]), _tile2(b_knorm_g[layer]),
        cosa, sina, cosb, sinb, b, s)

    oa = _attn_a(a_lq1[layer][None, :], a_lk1[layer][None, :], a_lq2[layer][None, :],
                 a_lk2[layer][None, :], a_subln_g[layer][None, :],
                 qa.reshape(b, s, A_W), kat, va.reshape(b, s, A_W))
    ob = _attn_b(qb.reshape(b, s, A_W), kbt, vb.reshape(b, s, B_KV_W))

    rwt = router_w[layer].T.astype(F32)
    rwh = rwt.astype(BF16)
    rwl = (rwt - rwh.astype(F32)).astype(BF16)
    x1, h2, topi, gates, rank, counts = _outproj(
        oa.reshape(n, A_W), ob.reshape(n, A_W), x2, w_out[layer],
        norm2_g[layer][None, :], rwh, rwl, router_b[layer].astype(F32)[:, None])

    counts = counts[:, 0]
    padded = ((counts + TM_MOE - 1) // TM_MOE) * TM_MOE
    pad_end = jnp.cumsum(padded)
    pad_start = pad_end - padded
    eids = jnp.arange(N_EXPERTS, dtype=jnp.int32)
    start_of = jnp.sum(jnp.where(topi[:, :, None] == eids, pad_start, 0), axis=-1)
    dest = start_of + rank
    n_tiles = n // T_ROWS
    dest_tiles = dest.reshape(TOP_K, n_tiles, T_ROWS).transpose(1, 0, 2).reshape(
        n_tiles, TOP_K * T_ROWS)
    n_slots = n * TOP_K + N_EXPERTS * TM_MOE
    n_blocks = n_slots // TM_MOE
    block_start = jnp.arange(n_blocks, dtype=jnp.int32) * TM_MOE
    block_expert = jnp.minimum(
        jnp.sum((block_start[:, None] >= pad_end[None, :]).astype(jnp.int32), axis=1),
        N_EXPERTS - 1)
    n_used = (pad_end[-1:] // TM_MOE).astype(jnp.int32)

    zero_lo = jnp.concatenate([jnp.maximum(pad_end - TM_MOE, pad_start), pad_end[-1:]])
    zero_hi = jnp.concatenate([pad_end, jnp.full((1,), n_slots, jnp.int32)])
    xs = _dispatch(zero_lo.astype(jnp.int32), zero_hi.astype(jnp.int32), dest_tiles, h2,
                   n_slots)
    ys = _moe(block_expert, n_used, xs,
              w_gate[layer], b_gate[layer][:, None, :],
              w_up[layer], b_up[layer][:, None, :],
              w_down[layer], b_down[layer][:, None, :])
    out = _combine(dest_tiles, x1, gates.T, ys)
    return out.reshape(b, s, d)
```
